```python
import jax, jax.numpy as jnp
from jax import lax
import numpy as np

D_MODEL = 1024
BATCH = 8
SEQ = 4096
DEPTH = 1

N_META = 16
CHUNK = 128
PAD = CHUNK - N_META
CONF_WIDTH = D_MODEL
CONF_KERNEL = 31
SSM_INNER = 2 * D_MODEL
SSM_HEAD_DIM = 64
SSM_HEADS = SSM_INNER // SSM_HEAD_DIM
SSM_GROUPS = 4
SSM_STATE = 128
SSM_CONV = 7
XBC_WIDTH = SSM_INNER + 2 * SSM_GROUPS * SSM_STATE
N_BRANCHES = 2
N_EXPERTS = 16
CAPACITY_FACTOR = 2
EXPERT_FF = 2 * D_MODEL
OFF_CONF = 0
OFF_Z = OFF_CONF + 2 * CONF_WIDTH
OFF_XBC = OFF_Z + SSM_INNER
OFF_DT = OFF_XBC + XBC_WIDTH
OFF_GATE = OFF_DT + 2 * SSM_HEADS
IN_WIDTH = OFF_GATE + N_BRANCHES * D_MODEL
EPS = 1e-6

kernel_name = "hybrid_conformer_ssd_ecmoe_encoder"


def rms_norm(x, w):
    xf = x.astype(jnp.float32)
    y = xf * lax.rsqrt(jnp.mean(xf * xf, axis=-1, keepdims=True) + EPS)
    return (y * w.astype(jnp.float32)).astype(x.dtype)


def layer_norm(x, g, b):
    xf = x.astype(jnp.float32)
    mu = jnp.mean(xf, axis=-1, keepdims=True)
    var = jnp.mean(jnp.square(xf - mu), axis=-1, keepdims=True)
    y = (xf - mu) * lax.rsqrt(var + EPS)
    return (y * g.astype(jnp.float32) + b.astype(jnp.float32)).astype(x.dtype)


def depthwise_conv(x, w, b):
    y = lax.conv_general_dilated(x, w[:, None, :].astype(x.dtype), window_strides=(1,), padding='SAME',
                                 dimension_numbers=('NWC', 'WIO', 'NWC'), feature_group_count=x.shape[-1])
    return y + b.astype(x.dtype)


def pad_front(a):
    return jnp.pad(a, [(0, 0), (PAD, 0)] + [(0, 0)] * (a.ndim - 2))


def ssd_scan(x, dt, A, B, C):
    b, T, h, p = x.shape
    g, n = B.shape[2], B.shape[3]
    j = h // g
    c = T // CHUNK
    xd = (x * dt[..., None]).reshape(b, c, CHUNK, g, j, p)
    a = jnp.moveaxis((dt * A).reshape(b, c, CHUNK, g, j), 2, -1)
    a_cs = jnp.cumsum(a, axis=-1)
    Bc = B.reshape(b, c, CHUNK, g, n)
    Cc = C.reshape(b, c, CHUNK, g, n)
    seg = a_cs[..., :, None] - a_cs[..., None, :]
    mask = jnp.tril(jnp.ones((CHUNK, CHUNK), dtype=bool))
    decay = jnp.exp(jnp.where(mask, seg, -jnp.inf))
    cb = jnp.einsum('bclgn,bcsgn->bcgls', Cc, Bc)
    y_diag = jnp.einsum('bcgls,bcgjls,bcsgjp->bclgjp', cb, decay, xd)
    decay_to_end = jnp.exp(a_cs[..., -1:] - a_cs)
    states = jnp.einsum('bclgn,bcgjl,bclgjp->bcgjpn', Bc, decay_to_end, xd)
    chunk_decay = jnp.exp(a_cs[..., -1])

    def step(carry, inp):
        st, dec = inp
        return carry * dec[..., None, None] + st, carry

    init = jnp.zeros((b, g, j, p, n), dtype=states.dtype)
    _, prev = lax.scan(step, init, (jnp.moveaxis(states, 1, 0), jnp.moveaxis(chunk_decay, 1, 0)))
    prev = jnp.moveaxis(prev, 0, 1)
    y_off = jnp.einsum('bclgn,bcgjpn,bcgjl->bclgjp', Cc, prev, jnp.exp(a_cs))
    return (y_diag + y_off).reshape(b, T, h, p)


def bidir_ssd(xs, dt_raw, dt_bias, a_log, Bs, Cs, d_skip):
    f32 = jnp.float32
    xf, Bf, Cf = xs.astype(f32), Bs.astype(f32), Cs.astype(f32)
    dtr = dt_raw.astype(f32)
    dt_f = jax.nn.softplus(dtr[..., :SSM_HEADS] + dt_bias[0].astype(f32))
    dt_b = jax.nn.softplus(dtr[..., SSM_HEADS:] + dt_bias[1].astype(f32))
    A = -jnp.exp(a_log.astype(f32))
    xp, Bp, Cp = pad_front(xf), pad_front(Bf), pad_front(Cf)
    y_f = ssd_scan(xp, pad_front(dt_f), A[0], Bp, Cp)
    fl = lambda t: jnp.flip(t, axis=1)
    y_b = fl(ssd_scan(fl(xp), fl(pad_front(dt_b)), A[1], fl(Bp), fl(Cp)))
    y = (y_f + y_b)[:, PAD:] + d_skip.astype(f32)[:, None] * xf
    return y.astype(xs.dtype)


def expert_choice_ffn(xn, w_router, w_gate, w_up, w_down):
    b, T, d = xn.shape
    cap = CAPACITY_FACTOR * T // N_EXPERTS
    aff = jax.nn.softmax(jnp.einsum('btd,de->bte', xn, w_router).astype(jnp.float32), axis=-1)
    top_aff, top_idx = lax.top_k(jnp.swapaxes(aff, 1, 2), cap)
    bi = jnp.arange(b)[:, None, None]
    xg = xn[bi, top_idx]
    hg = jnp.einsum('becd,edf->becf', xg, w_gate)
    hu = jnp.einsum('becd,edf->becf', xg, w_up)
    y = jnp.einsum('becf,efd->becd', jax.nn.silu(hg) * hu, w_down)
    y = y * top_aff[..., None].astype(y.dtype)
    return jnp.zeros_like(xn).at[bi, top_idx].add(y)


def setup_inputs(seed: int = 0) -> dict:
    key = jax.random.key(seed)
    ks = jax.random.split(key, 24)
    f32 = jnp.float32
    nrm = lambda k, shape, scale: jax.random.normal(k, shape, f32) * scale
    gain = lambda k, shape: 1.0 + 0.02 * jax.random.normal(k, shape, f32)
    L = DEPTH
    dt0 = jnp.exp(jax.random.uniform(ks[10], (L, 2, SSM_HEADS), f32, np.log(1e-3), np.log(1e-1)))
    dt_bias = dt0 + jnp.log(-jnp.expm1(-dt0))
    a_log = jnp.log(jax.random.uniform(ks[11], (L, 2, SSM_HEADS), f32, 1.0, 16.0))
    return {
        "x": jax.random.normal(ks[0], (BATCH, SEQ, D_MODEL), f32),
        "meta_tokens": nrm(ks[1], (N_META, D_MODEL), 1.0),
        "w_norm_mix": gain(ks[2], (L, D_MODEL)),
        "w_in": nrm(ks[3], (L, D_MODEL, IN_WIDTH), D_MODEL ** -0.5),
        "w_conf_dw": nrm(ks[4], (L, CONF_KERNEL, CONF_WIDTH), CONF_KERNEL ** -0.5),
        "b_conf_dw": nrm(ks[5], (L, CONF_WIDTH), 0.02),
        "conf_ln_g": gain(ks[6], (L, CONF_WIDTH)),
        "conf_ln_b": nrm(ks[7], (L, CONF_WIDTH), 0.02),
        "w_conf_out": nrm(ks[8], (L, CONF_WIDTH, D_MODEL), CONF_WIDTH ** -0.5),
        "w_ssm_conv": nrm(ks[9], (L, SSM_CONV, XBC_WIDTH), SSM_CONV ** -0.5),
        "b_ssm_conv": nrm(ks[12], (L, XBC_WIDTH), 0.02),
        "ssm_dt_bias": dt_bias,
        "ssm_a_log": a_log,
        "ssm_d": gain(ks[13], (L, SSM_HEADS)),
        "w_ssm_norm": gain(ks[14], (L, SSM_INNER)),
        "w_ssm_out": nrm(ks[15], (L, SSM_INNER, D_MODEL), SSM_INNER ** -0.5),
        "w_out": nrm(ks[16], (L, D_MODEL, D_MODEL), D_MODEL ** -0.5),
        "w_norm_ffn": gain(ks[17], (L, D_MODEL)),
        "w_router": nrm(ks[18], (L, D_MODEL, N_EXPERTS), D_MODEL ** -0.5),
        "w_exp_gate": nrm(ks[19], (L, N_EXPERTS, D_MODEL, EXPERT_FF), D_MODEL ** -0.5),
        "w_exp_up": nrm(ks[20], (L, N_EXPERTS, D_MODEL, EXPERT_FF), D_MODEL ** -0.5),
        "w_exp_down": nrm(ks[21], (L, N_EXPERTS, EXPERT_FF, D_MODEL), EXPERT_FF ** -0.5),
        "w_norm_final": gain(ks[22], (D_MODEL,)),
    }


def reference(x, meta_tokens, w_norm_mix, w_in, w_conf_dw, b_conf_dw, conf_ln_g, conf_ln_b, w_conf_out,
              w_ssm_conv, b_ssm_conv, ssm_dt_bias, ssm_a_log, ssm_d, w_ssm_norm, w_ssm_out, w_out,
              w_norm_ffn, w_router, w_exp_gate, w_exp_up, w_exp_down, w_norm_final):
    b = x.shape[0]
    meta = jnp.broadcast_to(meta_tokens[None].astype(x.dtype), (b, N_META, D_MODEL))
    h = jnp.concatenate([meta, x], axis=1)
    Lt = h.shape[1]
    for l in range(DEPTH):
        u = rms_norm(h, w_norm_mix[l])
        proj = jnp.einsum('btd,dk->btk', u, w_in[l])
        conf_a = proj[..., OFF_CONF:OFF_CONF + CONF_WIDTH]
        conf_g = proj[..., OFF_CONF + CONF_WIDTH:OFF_Z]
        z = proj[..., OFF_Z:OFF_XBC]
        xbc = proj[..., OFF_XBC:OFF_DT]
        dt_raw = proj[..., OFF_DT:OFF_GATE]
        gates = jax.nn.sigmoid(proj[..., OFF_GATE:].astype(jnp.float32)).astype(h.dtype)

        c = conf_a * jax.nn.sigmoid(conf_g)
        c = depthwise_conv(c, w_conf_dw[l], b_conf_dw[l])
        c = jax.nn.silu(layer_norm(c, conf_ln_g[l], conf_ln_b[l]))
        branch_conf = jnp.einsum('btc,cd->btd', c, w_conf_out[l])

        xbc = jax.nn.silu(depthwise_conv(xbc, w_ssm_conv[l], b_ssm_conv[l]))
        xs = xbc[..., :SSM_INNER].reshape(b, Lt, SSM_HEADS, SSM_HEAD_DIM)
        Bs = xbc[..., SSM_INNER:SSM_INNER + SSM_GROUPS * SSM_STATE].reshape(b, Lt, SSM_GROUPS, SSM_STATE)
        Cs = xbc[..., SSM_INNER + SSM_GROUPS * SSM_STATE:].reshape(b, Lt, SSM_GROUPS, SSM_STATE)
        y = bidir_ssd(xs, dt_raw, ssm_dt_bias[l], ssm_a_log[l], Bs, Cs, ssm_d[l]).reshape(b, Lt, SSM_INNER)
        y = rms_norm(y * jax.nn.silu(z), w_ssm_norm[l])
        branch_ssm = jnp.einsum('btc,cd->btd', y, w_ssm_out[l])

        merged = gates[..., :D_MODEL] * branch_conf + gates[..., D_MODEL:] * branch_ssm
        h = h + jnp.einsum('btd,de->bte', merged, w_out[l])

        hn = rms_norm(h, w_norm_ffn[l])
        h = h + expert_choice_ffn(hn, w_router[l], w_exp_gate[l], w_exp_up[l], w_exp_down[l])
    return rms_norm(h[:, N_META:], w_norm_final)
```

```python
import functools

import jax
import jax.numpy as jnp
from jax import lax
from jax.experimental import pallas as pl
from jax.experimental.pallas import tpu as pltpu

CHUNK = 128
SSM_STATE = 128
SSM_HEAD_DIM = 64
SSM_GROUPS = 4
CAPACITY_FACTOR = 2
EPS = 1e-6
V7X_VMEM_LIMIT = 52 * 1024 * 1024
F32 = jnp.float32
BF16 = jnp.bfloat16

_NT = (((1,), (1,)), ((), ()))


def _params(*sem):
    return pltpu.CompilerParams(dimension_semantics=sem, vmem_limit_bytes=V7X_VMEM_LIMIT)


def _tile(n, cap, mult=128):
    best = None
    for t in range(mult, min(n, cap) + 1, mult):
        if n % t == 0:
            best = t
    assert best is not None, (n, cap, mult)
    return best


def _sigmoid(v):
    return jax.nn.sigmoid(v)


def _prep_kernel(x_ref, meta_ref, w_ref, u_ref, h_ref, *, n_meta):
    c = pl.program_id(1)
    w = w_ref[...]

    def emit(rows):
        h_ref[0] = rows
        ms = jnp.mean(rows * rows, axis=-1, keepdims=True)
        u_ref[0] = (rows * lax.rsqrt(ms + EPS) * w).astype(u_ref.dtype)

    @pl.when(c == 0)
    def _():
        d = meta_ref.shape[1]
        emit(jnp.concatenate([jnp.zeros((CHUNK - n_meta, d), F32), meta_ref[...]], axis=0))

    @pl.when(c > 0)
    def _():
        emit(x_ref[0])


def _prep(x, meta, w_norm):
    b, seq, d = x.shape
    n_meta = meta.shape[0]
    nch = seq // CHUNK + 1
    tp = nch * CHUNK
    return pl.pallas_call(
        functools.partial(_prep_kernel, n_meta=n_meta),
        grid=(b, nch),
        in_specs=[
            pl.BlockSpec((1, CHUNK, d), lambda i, c: (i, jnp.maximum(c - 1, 0), 0)),
            pl.BlockSpec((n_meta, d), lambda i, c: (0, 0)),
            pl.BlockSpec((1, d), lambda i, c: (0, 0)),
        ],
        out_specs=[
            pl.BlockSpec((1, CHUNK, d), lambda i, c: (i, c, 0)),
            pl.BlockSpec((1, CHUNK, d), lambda i, c: (i, c, 0)),
        ],
        out_shape=[jax.ShapeDtypeStruct((b, tp, d), BF16), jax.ShapeDtypeStruct((b, tp, d), F32)],
        compiler_params=_params("parallel", "arbitrary"),
        name="prep",
    )(x, meta, w_norm.reshape(1, d))


def _mm_kernel(u_ref, w_ref, o_ref, *, act):
    acc = jnp.dot(u_ref[...], w_ref[...], preferred_element_type=F32)
    if act == "silu":
        acc = acc * _sigmoid(acc)
    elif act == "sigmoid":
        acc = _sigmoid(acc)
    o_ref[...] = acc.astype(o_ref.dtype)


def _mm(u2, w, act, out_dtype=F32, tn=512):
    r, k = u2.shape
    n = w.shape[1]
    tm = _tile(r, 1024)
    tn = _tile(n, tn)
    return pl.pallas_call(
        functools.partial(_mm_kernel, act=act),
        grid=(r // tm, n // tn),
        in_specs=[pl.BlockSpec((tm, k), lambda i, j: (i, 0)), pl.BlockSpec((k, tn), lambda i, j: (0, j))],
        out_specs=pl.BlockSpec((tm, tn), lambda i, j: (i, j)),
        out_shape=jax.ShapeDtypeStruct((r, n), out_dtype),
        compiler_params=_params("parallel", "arbitrary"),
        name="proj_" + str(act),
    )(u2, w)


def _glu_kernel(u_ref, wa_ref, wg_ref, o_ref):
    u = u_ref[...]
    a = jnp.dot(u, wa_ref[...], preferred_element_type=F32)
    g = jnp.dot(u, wg_ref[...], preferred_element_type=F32)
    o_ref[...] = a * _sigmoid(g)


def _glu(u2, wa, wg, tn=512):
    r, k = u2.shape
    n = wa.shape[1]
    tm = _tile(r, 1024)
    tn = _tile(n, tn)
    return pl.pallas_call(
        _glu_kernel,
        grid=(r // tm, n // tn),
        in_specs=[
            pl.BlockSpec((tm, k), lambda i, j: (i, 0)),
            pl.BlockSpec((k, tn), lambda i, j: (0, j)),
            pl.BlockSpec((k, tn), lambda i, j: (0, j)),
        ],
        out_specs=pl.BlockSpec((tm, tn), lambda i, j: (i, j)),
        out_shape=jax.ShapeDtypeStruct((r, n), F32),
        compiler_params=_params("parallel", "arbitrary"),
        name="proj_glu",
    )(u2, wa, wg)


def _dt_kernel(u_ref, wt_ref, bias_ref, o_ref, *, pad):
    raw = lax.dot_general(wt_ref[...], u_ref[0], _NT, preferred_element_type=F32)
    v = raw + bias_ref[...]
    sp = jnp.maximum(v, 0.0) + jnp.log1p(jnp.exp(-jnp.abs(v)))
    t = lax.broadcasted_iota(jnp.int32, sp.shape, 1)
    o_ref[0] = jnp.where(t >= pad, sp, 0.0)


def _dt_proj(u3, w_dt_t, bias_col, pad):
    b, tp, d = u3.shape
    h2 = w_dt_t.shape[0]
    return pl.pallas_call(
        functools.partial(_dt_kernel, pad=pad),
        grid=(b,),
        in_specs=[
            pl.BlockSpec((1, tp, d), lambda i: (i, 0, 0)),
            pl.BlockSpec((h2, d), lambda i: (0, 0)),
            pl.BlockSpec((h2, 1), lambda i: (0, 0)),
        ],
        out_specs=pl.BlockSpec((1, h2, tp), lambda i: (i, 0, 0)),
        out_shape=jax.ShapeDtypeStruct((b, h2, tp), F32),
        compiler_params=_params("parallel"),
        name="proj_dt",
    )(u3, w_dt_t, bias_col)


_CONF_HALO = 16
_ROW_TILE = 32
_LANE_TILE = 256


def _conf_kernel(cur_ref, prev_ref, next_ref, wdw_ref, bdw_ref, g_ref, b_ref, wout_ref, gate_ref, o_ref,
                 win_ref, sh_ref, conv_ref, *, ktaps):
    i = pl.program_id(1)
    n_i = pl.num_programs(1)
    tm, cw = cur_ref.shape[1], cur_ref.shape[2]
    hl = _CONF_HALO
    half = (ktaps - 1) // 2
    ext = sh_ref.shape[1]
    win_ref[0:hl, :] = jnp.where(i > 0, prev_ref[0], 0.0)
    win_ref[hl:hl + tm, :] = cur_ref[0]
    win_ref[hl + tm:hl + tm + hl, :] = jnp.where(i < n_i - 1, next_ref[0], 0.0)
    for r in range(8):
        sh_ref[r] = win_ref[r:r + ext, :]

    def row_tile(rt, carry):
        base = pl.multiple_of(rt * _ROW_TILE, _ROW_TILE)
        for lt in range(cw // _LANE_TILE):
            ls = slice(lt * _LANE_TILE, (lt + 1) * _LANE_TILE)
            acc = jnp.broadcast_to(bdw_ref[:, ls], (_ROW_TILE, _LANE_TILE))
            for k in range(ktaps):
                q, r = divmod(hl - half + k, 8)
                acc = acc + sh_ref[r, pl.ds(base + 8 * q, _ROW_TILE), ls] * wdw_ref[k:k + 1, ls]
            conv_ref[pl.ds(base, _ROW_TILE), ls] = acc
        return carry

    lax.fori_loop(0, tm // _ROW_TILE, row_tile, 0)
    cv = conv_ref[...]
    mu = jnp.mean(cv, axis=-1, keepdims=True)
    xc = cv - mu
    var = jnp.mean(xc * xc, axis=-1, keepdims=True)
    y = xc * lax.rsqrt(var + EPS) * g_ref[...] + b_ref[...]
    y = y * _sigmoid(y)
    br = jnp.dot(y.astype(BF16), wout_ref[...], preferred_element_type=F32)
    o_ref[0] = gate_ref[0] * br


def _conf_branch(c3, w_dw, b_dw, ln_g, ln_b, w_out_bf, gates3):
    b, tp, cw = c3.shape
    d = w_out_bf.shape[1]
    ktaps = w_dw.shape[0]
    assert (ktaps - 1) // 2 <= _CONF_HALO
    tm = _tile(tp, 256)
    hl = _CONF_HALO
    nh = tm // hl
    n_i = tp // tm
    ext = tm + hl + 8
    return pl.pallas_call(
        functools.partial(_conf_kernel, ktaps=ktaps),
        grid=(b, n_i),
        in_specs=[
            pl.BlockSpec((1, tm, cw), lambda bi, i: (bi, i, 0)),
            pl.BlockSpec((1, hl, cw), lambda bi, i: (bi, jnp.maximum(i * nh - 1, 0), 0)),
            pl.BlockSpec((1, hl, cw), lambda bi, i: (bi, jnp.minimum((i + 1) * nh, tp // hl - 1), 0)),
            pl.BlockSpec((ktaps, cw), lambda bi, i: (0, 0)),
            pl.BlockSpec((1, cw), lambda bi, i: (0, 0)),
            pl.BlockSpec((1, cw), lambda bi, i: (0, 0)),
            pl.BlockSpec((1, cw), lambda bi, i: (0, 0)),
            pl.BlockSpec((cw, d), lambda bi, i: (0, 0)),
            pl.BlockSpec((1, tm, d), lambda bi, i: (bi, i, 0)),
        ],
        out_specs=pl.BlockSpec((1, tm, d), lambda bi, i: (bi, i, 0)),
        out_shape=jax.ShapeDtypeStruct((b, tp, d), F32),
        scratch_shapes=[
            pltpu.VMEM((tm + 2 * hl, cw), F32),
            pltpu.VMEM((8, ext, cw), F32),
            pltpu.VMEM((tm, cw), F32),
        ],
        compiler_params=_params("parallel", "arbitrary"),
        name="conf_branch",
    )(c3, c3, c3, w_dw, b_dw.reshape(1, cw), ln_g.reshape(1, cw), ln_b.reshape(1, cw), w_out_bf, gates3)


_SSM_HALO = 8


def _ssm_conv_kernel(cur_ref, prev_ref, next_ref, w_ref, b_ref, xs_ref, xst_ref, bm_ref, cm_ref,
                     win_ref, conv_ref, *, ktaps, pad, inner):
    c = pl.program_id(1)
    n_c = pl.num_programs(1)
    tm, xw = cur_ref.shape[1], cur_ref.shape[2]
    hl = _SSM_HALO
    half = (ktaps - 1) // 2
    win_ref[0:hl, :] = jnp.where(c > 0, prev_ref[0], 0.0)
    win_ref[hl:hl + tm, :] = cur_ref[0]
    win_ref[hl + tm:hl + tm + hl, :] = jnp.where(c < n_c - 1, next_ref[0], 0.0)
    for rt in range(tm // _ROW_TILE):
        r0 = rt * _ROW_TILE
        t = c * tm + r0 + lax.broadcasted_iota(jnp.int32, (_ROW_TILE, 1), 0)
        valid = t >= pad
        for lt in range(xw // _LANE_TILE):
            ls = slice(lt * _LANE_TILE, (lt + 1) * _LANE_TILE)
            acc = jnp.broadcast_to(b_ref[:, ls], (_ROW_TILE, _LANE_TILE))
            for k in range(ktaps):
                o = r0 + hl - half + k
                acc = acc + win_ref[o:o + _ROW_TILE, ls] * w_ref[k:k + 1, ls]
            acc = acc * _sigmoid(acc)
            conv_ref[r0:r0 + _ROW_TILE, ls] = jnp.where(valid, acc, 0.0)
    xs = conv_ref[:, 0:inner]
    xs_ref[0] = xs
    xst_ref[0] = xs.T
    gn = (xw - inner) // 2
    bm_ref[0] = conv_ref[:, inner:inner + gn].astype(BF16)
    cm_ref[0] = conv_ref[:, inner + gn:inner + 2 * gn].astype(BF16)


def _ssm_conv(xbc3, w, bias, pad, inner):
    b, tp, xw = xbc3.shape
    ktaps = w.shape[0]
    tm = CHUNK
    hl = _SSM_HALO
    nh = tm // hl
    gn = (xw - inner) // 2
    return pl.pallas_call(
        functools.partial(_ssm_conv_kernel, ktaps=ktaps, pad=pad, inner=inner),
        grid=(b, tp // tm),
        in_specs=[
            pl.BlockSpec((1, tm, xw), lambda bi, i: (bi, i, 0)),
            pl.BlockSpec((1, hl, xw), lambda bi, i: (bi, jnp.maximum(i * nh - 1, 0), 0)),
            pl.BlockSpec((1, hl, xw), lambda bi, i: (bi, jnp.minimum((i + 1) * nh, tp // hl - 1), 0)),
            pl.BlockSpec((ktaps, xw), lambda bi, i: (0, 0)),
            pl.BlockSpec((1, xw), lambda bi, i: (0, 0)),
        ],
        out_specs=[
            pl.BlockSpec((1, tm, inner), lambda bi, i: (bi, i, 0)),
            pl.BlockSpec((1, inner, tm), lambda bi, i: (bi, 0, i)),
            pl.BlockSpec((1, tm, gn), lambda bi, i: (bi, i, 0)),
            pl.BlockSpec((1, tm, gn), lambda bi, i: (bi, i, 0)),
        ],
        out_shape=[
            jax.ShapeDtypeStruct((b, tp, inner), F32),
            jax.ShapeDtypeStruct((b, inner, tp), F32),
            jax.ShapeDtypeStruct((b, tp, gn), BF16),
            jax.ShapeDtypeStruct((b, tp, gn), BF16),
        ],
        scratch_shapes=[pltpu.VMEM((tm + 2 * hl, xw), F32), pltpu.VMEM((tm, xw), F32)],
        compiler_params=_params("parallel", "arbitrary"),
        name="ssm_conv",
    )(xbc3, xbc3, xbc3, w, bias.reshape(1, xw))


def _ssd_kernel(xst_ref, dtt_ref, bm_ref, cm_ref, a_ref, y_ref, s_ref, *, reverse, heads, groups):
    c = pl.program_id(1)
    p, n = SSM_HEAD_DIM, SSM_STATE
    hpg = heads // groups

    @pl.when(c == 0)
    def _():
        s_ref[...] = jnp.zeros_like(s_ref)

    dtt = dtt_ref[0]
    at = dtt * a_ref[...]
    li = lax.broadcasted_iota(jnp.int32, (CHUNK, CHUNK), 0)
    ri = lax.broadcasted_iota(jnp.int32, (CHUNK, CHUNK), 1)
    keep = (ri >= li) if reverse else (ri <= li)
    mx = keep.astype(F32)
    hi = lax.Precision.HIGHEST
    cum = lax.dot_general(mx, at, _NT, precision=hi, preferred_element_type=F32)
    cumt = lax.dot_general(at, mx, _NT, precision=hi, preferred_element_type=F32)
    edge = 0 if reverse else CHUNK - 1
    tott = cumt[:, edge:edge + 1]
    wt = dtt * jnp.exp(tott - cumt)
    ecum = jnp.exp(cum)
    etot = jnp.exp(tott)

    for g in range(groups):
        cg = cm_ref[0, :, g * n:(g + 1) * n]
        bg = bm_ref[0, :, g * n:(g + 1) * n]
        cb = lax.dot_general(cg, bg, _NT, preferred_element_type=F32)
        cg32 = cg.astype(F32)
        for j in range(hpg):
            h = g * hpg + j
            seg = cum[:, h:h + 1] - cumt[h:h + 1, :]
            decay = jnp.exp(jnp.where(keep, seg, -jnp.inf))
            m = (cb * decay).astype(BF16)
            ce = (cg32 * ecum[:, h:h + 1]).astype(BF16)
            lhs = jnp.concatenate([m, ce], axis=1)
            xt = xst_ref[0, h * p:(h + 1) * p, :]
            sh = s_ref[h]
            rhs_t = jnp.concatenate([(xt * dtt[h:h + 1, :]).astype(BF16), sh.astype(BF16)], axis=1)
            y_ref[0, :, h * p:(h + 1) * p] = lax.dot_general(lhs, rhs_t, _NT, preferred_element_type=F32)
            xw = (xt * wt[h:h + 1, :]).astype(BF16)
            s_ref[h] = sh * etot[h:h + 1, :] + jnp.dot(xw, bg, preferred_element_type=F32)


def _ssd(xst, dtt_all, bm, cm, a_col_all, direction, heads):
    b, hp, tp = xst.shape
    nch = tp // CHUNK
    gn = bm.shape[2]
    reverse = direction == 1
    cidx = (lambda c: nch - 1 - c) if reverse else (lambda c: c)
    return pl.pallas_call(
        functools.partial(_ssd_kernel, reverse=reverse, heads=heads, groups=SSM_GROUPS),
        grid=(b, nch),
        in_specs=[
            pl.BlockSpec((1, hp, CHUNK), lambda i, c: (i, 0, cidx(c))),
            pl.BlockSpec((1, heads, CHUNK), lambda i, c: (i, direction, cidx(c))),
            pl.BlockSpec((1, CHUNK, gn), lambda i, c: (i, cidx(c), 0)),
            pl.BlockSpec((1, CHUNK, gn), lambda i, c: (i, cidx(c), 0)),
            pl.BlockSpec((heads, 1), lambda i, c: (direction, 0)),
        ],
        out_specs=pl.BlockSpec((1, CHUNK, hp), lambda i, c: (i, cidx(c), 0)),
        out_shape=jax.ShapeDtypeStruct((b, tp, hp), F32),
        scratch_shapes=[pltpu.VMEM((heads, SSM_HEAD_DIM, SSM_STATE), F32)],
        compiler_params=_params("parallel", "arbitrary"),
        name="ssd_rev" if reverse else "ssd_fwd",
    )(xst, dtt_all, bm, cm, a_col_all)


def _merge_kernel(yf_ref, yb_ref, xs_ref, zs_ref, dsk_ref, wsn_ref, wso_ref, gc_ref, g1_ref, wo_ref, h0_ref,
                  wnf_ref, wrt_ref, h1_ref, aff_ref):
    y = yf_ref[...] + yb_ref[...] + dsk_ref[...] * xs_ref[...]
    y = y * zs_ref[...]
    ms = jnp.mean(y * y, axis=-1, keepdims=True)
    yn = (y * lax.rsqrt(ms + EPS) * wsn_ref[...]).astype(BF16)
    bs = jnp.dot(yn, wso_ref[...], preferred_element_type=F32)
    merged = gc_ref[...] + g1_ref[...] * bs
    h1 = h0_ref[...] + jnp.dot(merged.astype(BF16), wo_ref[...], preferred_element_type=F32)
    h1_ref[...] = h1
    ms1 = jnp.mean(h1 * h1, axis=-1, keepdims=True)
    hn = (h1 * lax.rsqrt(ms1 + EPS) * wnf_ref[...]).astype(BF16)
    logits = lax.dot_general(wrt_ref[...], hn, _NT, preferred_element_type=F32)
    mx = jnp.max(logits, axis=0, keepdims=True)
    ex = jnp.exp(logits - mx)
    aff_ref[...] = ex / jnp.sum(ex, axis=0, keepdims=True)


def _merge(yf, yb, xs, zs, dskip, w_ssm_norm, w_ssm_out_bf, gc, gates, w_out_bf, h0, w_norm_ffn, w_router_t_bf):
    r, inner = yf.shape
    d = h0.shape[1]
    e = w_router_t_bf.shape[0]
    tm = _tile(r, 256)
    row = lambda i: (i, 0)
    fix = lambda i: (0, 0)
    return pl.pallas_call(
        _merge_kernel,
        grid=(r // tm,),
        in_specs=[
            pl.BlockSpec((tm, inner), row), pl.BlockSpec((tm, inner), row), pl.BlockSpec((tm, inner), row),
            pl.BlockSpec((tm, inner), row), pl.BlockSpec((1, inner), fix), pl.BlockSpec((1, inner), fix),
            pl.BlockSpec((inner, d), fix), pl.BlockSpec((tm, d), row), pl.BlockSpec((tm, d), lambda i: (i, 1)),
            pl.BlockSpec((d, d), fix), pl.BlockSpec((tm, d), row), pl.BlockSpec((1, d), fix),
            pl.BlockSpec((e, d), fix),
        ],
        out_specs=[pl.BlockSpec((tm, d), row), pl.BlockSpec((e, tm), lambda i: (0, i))],
        out_shape=[jax.ShapeDtypeStruct((r, d), F32), jax.ShapeDtypeStruct((e, r), F32)],
        compiler_params=_params("parallel"),
        name="merge_router",
    )(yf, yb, xs, zs, dskip, w_ssm_norm, w_ssm_out_bf, gc, gates, w_out_bf, h0, w_norm_ffn, w_router_t_bf)


def _select_kernel(aff_ref, idx_ref, wts_ref, rank_ref, *, pad, cap):
    e, tp = aff_ref.shape
    capp = idx_ref.shape[2]
    aff = aff_ref[...]
    tpos = lax.broadcasted_iota(jnp.int32, (e, tp), 1)
    bits = jnp.where(tpos >= pad, pltpu.bitcast(aff, jnp.int32), -1)

    def refine(i, thr):
        cand = thr | lax.shift_left(jnp.int32(1), 30 - i)
        cnt = jnp.sum((bits >= cand).astype(jnp.int32), axis=1, keepdims=True)
        return jnp.where(cnt >= cap, cand, thr)

    thr = lax.fori_loop(0, 31, refine, jnp.zeros((e, 1), jnp.int32))
    gt = bits > thr
    eq = bits == thr
    need = cap - jnp.sum(gt.astype(jnp.int32), axis=1, keepdims=True)

    ri = lax.broadcasted_iota(jnp.int32, (CHUNK, CHUNK), 0)
    ci = lax.broadcasted_iota(jnp.int32, (CHUNK, CHUNK), 1)
    upper = (ri <= ci).astype(BF16)

    def prefix(flags):
        off = jnp.zeros((e, 1), F32)
        for k in range(tp // CHUNK):
            blk = flags[:, k * CHUNK:(k + 1) * CHUNK].astype(BF16)
            inc = jnp.dot(blk, upper, preferred_element_type=F32) + off
            rank_ref[:, k * CHUNK:(k + 1) * CHUNK] = inc
            off = inc[:, CHUNK - 1:CHUNK]

    eqf = jnp.where(eq, 1.0, 0.0)
    prefix(eqf)
    sel = gt | (eq & ((rank_ref[...] - eqf) < need.astype(F32)))
    prefix(jnp.where(sel, 1.0, 0.0))
    rank_ref[...] = jnp.where(sel, rank_ref[...], 0.0)
    tpos_e = lax.broadcasted_iota(jnp.int32, (1, tp), 1).astype(F32)

    for ei in range(e):
        slot_e = rank_ref[ei:ei + 1, :]
        aff_e = aff_ref[ei:ei + 1, :]

        def emit(jb, carry):
            j0 = pl.multiple_of(jb * 8, 8)
            want = (j0 + 1 + lax.broadcasted_iota(jnp.int32, (8, 1), 0)).astype(F32)
            hit = slot_e == want
            idx_ref[0, ei, pl.ds(j0, 8), :] = jnp.sum(jnp.where(hit, tpos_e, 0.0), axis=1, keepdims=True).astype(jnp.int32)
            wts_ref[0, ei, pl.ds(j0, 8), :] = jnp.sum(jnp.where(hit, aff_e, 0.0), axis=1, keepdims=True)
            return carry

        lax.fori_loop(0, capp // 8, emit, 0)


def _select(aff_t, b, tp, pad, cap, capp):
    e = aff_t.shape[0]
    return pl.pallas_call(
        functools.partial(_select_kernel, pad=pad, cap=cap),
        grid=(b,),
        in_specs=[pl.BlockSpec((e, tp), lambda i: (0, i))],
        out_specs=[
            pl.BlockSpec((1, e, capp, 1), lambda i: (i, 0, 0, 0)),
            pl.BlockSpec((1, e, capp, 1), lambda i: (i, 0, 0, 0)),
        ],
        out_shape=[jax.ShapeDtypeStruct((b, e, capp, 1), jnp.int32), jax.ShapeDtypeStruct((b, e, capp, 1), F32)],
        scratch_shapes=[pltpu.VMEM((e, tp), F32)],
        compiler_params=_params("parallel"),
        name="select",
    )(aff_t)


def _gather_kernel(idx_ref, h1_ref, wn_ref, xn_ref, rows_ref, *, n_exp):
    bi, ei = pl.program_id(0), pl.program_id(1)
    capp = rows_ref.shape[0]
    base = (bi * n_exp + ei) * capp

    def copy(j, carry):
        t = idx_ref[base + j]
        rows_ref[pl.ds(j, 1), :] = h1_ref[0, pl.ds(t, 1), :]
        return carry

    lax.fori_loop(0, capp, copy, 0, unroll=8)
    rows = rows_ref[...]
    ms = jnp.mean(rows * rows, axis=-1, keepdims=True)
    xn_ref[0, 0] = (rows * lax.rsqrt(ms + EPS) * wn_ref[...]).astype(BF16)


def _gather(idx_flat, h1_3, w_norm, n_exp, capp):
    b, tp, d = h1_3.shape
    return pl.pallas_call(
        functools.partial(_gather_kernel, n_exp=n_exp),
        grid_spec=pltpu.PrefetchScalarGridSpec(
            num_scalar_prefetch=1,
            grid=(b, n_exp),
            in_specs=[
                pl.BlockSpec((1, tp, d), lambda i, e, idx: (i, 0, 0)),
                pl.BlockSpec((1, d), lambda i, e, idx: (0, 0)),
            ],
            out_specs=pl.BlockSpec((1, 1, capp, d), lambda i, e, idx: (i, e, 0, 0)),
            scratch_shapes=[pltpu.VMEM((capp, d), F32)],
        ),
        out_shape=jax.ShapeDtypeStruct((b, n_exp, capp, d), BF16),
        compiler_params=_params("parallel", "arbitrary"),
        name="moe_gather",
    )(idx_flat, h1_3, w_norm)


def _ffn_kernel(xn_ref, wts_ref, wg_ref, wu_ref, wd_ref, y_ref):
    f = pl.program_id(2)
    nb, capp, d = xn_ref.shape[0], xn_ref.shape[2], xn_ref.shape[3]
    xn = xn_ref[...].reshape(nb * capp, d)
    hg = jnp.dot(xn, wg_ref[0].astype(BF16), preferred_element_type=F32)
    hu = jnp.dot(xn, wu_ref[0].astype(BF16), preferred_element_type=F32)
    act = (hg * _sigmoid(hg) * hu).astype(BF16)
    yp = jnp.dot(act, wd_ref[0].astype(BF16), preferred_element_type=F32).reshape(nb, 1, capp, d)

    @pl.when(f == 0)
    def _():
        y_ref[...] = yp

    @pl.when(f > 0)
    def _():
        y_ref[...] += yp

    @pl.when(f == pl.num_programs(2) - 1)
    def _():
        y_ref[...] = y_ref[...] * wts_ref[...]


def _ffn(xn, wts, w_gate, w_up, w_down):
    b, n_exp, capp, d = xn.shape
    ff = w_gate.shape[2]
    nb = 2 if b % 2 == 0 else 1
    tf = _tile(ff, 512)
    return pl.pallas_call(
        _ffn_kernel,
        grid=(n_exp, b // nb, ff // tf),
        in_specs=[
            pl.BlockSpec((nb, 1, capp, d), lambda e, m, f: (m, e, 0, 0)),
            pl.BlockSpec((nb, 1, capp, 1), lambda e, m, f: (m, e, 0, 0)),
            pl.BlockSpec((1, d, tf), lambda e, m, f: (e, 0, f)),
            pl.BlockSpec((1, d, tf), lambda e, m, f: (e, 0, f)),
            pl.BlockSpec((1, tf, d), lambda e, m, f: (e, f, 0)),
        ],
        out_specs=pl.BlockSpec((nb, 1, capp, d), lambda e, m, f: (m, e, 0, 0)),
        out_shape=jax.ShapeDtypeStruct((b, n_exp, capp, d), F32),
        compiler_params=_params("parallel", "parallel", "arbitrary"),
        name="moe_ffn",
    )(xn, wts, w_gate, w_up, w_down)


def _scatter_kernel(idx_ref, y_ref, o_ref, *, n_exp):
    bi, ei = pl.program_id(0), pl.program_id(1)
    capp = y_ref.shape[2]
    base = (bi * n_exp + ei) * capp

    @pl.when(ei == 0)
    def _():
        o_ref[...] = jnp.zeros_like(o_ref)

    def add(j, carry):
        t = idx_ref[base + j]
        o_ref[0, pl.ds(t, 1), :] = o_ref[0, pl.ds(t, 1), :] + y_ref[0, 0, pl.ds(j, 1), :]
        return carry

    lax.fori_loop(0, capp, add, 0)


def _scatter(idx_flat, y, tp):
    b, n_exp, capp, d = y.shape
    return pl.pallas_call(
        functools.partial(_scatter_kernel, n_exp=n_exp),
        grid_spec=pltpu.PrefetchScalarGridSpec(
            num_scalar_prefetch=1,
            grid=(b, n_exp),
            in_specs=[pl.BlockSpec((1, 1, capp, d), lambda i, e, idx: (i, e, 0, 0))],
            out_specs=pl.BlockSpec((1, tp, d), lambda i, e, idx: (i, 0, 0)),
        ),
        out_shape=jax.ShapeDtypeStruct((b, tp, d), F32),
        compiler_params=_params("parallel", "arbitrary"),
        name="moe_scatter",
    )(idx_flat, y)


def _final_kernel(h1_ref, dl_ref, w_ref, o_ref):
    h = h1_ref[0] + dl_ref[0]
    ms = jnp.mean(h * h, axis=-1, keepdims=True)
    o_ref[0] = h * lax.rsqrt(ms + EPS) * w_ref[...]


def _final(h1_3, delta, w_norm):
    b, tp, d = h1_3.shape
    seq = tp - CHUNK
    return pl.pallas_call(
        _final_kernel,
        grid=(b, seq // CHUNK),
        in_specs=[
            pl.BlockSpec((1, CHUNK, d), lambda i, c: (i, c + 1, 0)),
            pl.BlockSpec((1, CHUNK, d), lambda i, c: (i, c + 1, 0)),
            pl.BlockSpec((1, d), lambda i, c: (0, 0)),
        ],
        out_specs=pl.BlockSpec((1, CHUNK, d), lambda i, c: (i, c, 0)),
        out_shape=jax.ShapeDtypeStruct((b, seq, d), F32),
        compiler_params=_params("parallel", "parallel"),
        name="final_norm",
    )(h1_3, delta, w_norm)


def kernel(x, meta_tokens, w_norm_mix, w_in, w_conf_dw, b_conf_dw, conf_ln_g, conf_ln_b, w_conf_out,
           w_ssm_conv, b_ssm_conv, ssm_dt_bias, ssm_a_log, ssm_d, w_ssm_norm, w_ssm_out, w_out,
           w_norm_ffn, w_router, w_exp_gate, w_exp_up, w_exp_down, w_norm_final):
    b, seq, d = x.shape
    depth = w_in.shape[0]
    assert depth == 1 and seq % CHUNK == 0
    n_meta = meta_tokens.shape[0]
    pad = CHUNK - n_meta
    lt = n_meta + seq
    tp = seq + CHUNK
    r = b * tp
    cw = w_conf_dw.shape[2]
    heads = ssm_d.shape[1]
    inner = heads * SSM_HEAD_DIM
    xw = w_ssm_conv.shape[2]
    n_exp = w_router.shape[2]
    cap = CAPACITY_FACTOR * lt // n_exp
    capp = -(-cap // 8) * 8
    off_z = 2 * cw
    off_xbc = off_z + inner
    off_dt = off_xbc + xw
    off_gate = off_dt + 2 * heads

    w_in0 = w_in[0]
    wa = w_in0[:, 0:cw].astype(BF16)
    wg = w_in0[:, cw:off_z].astype(BF16)
    wz = w_in0[:, off_z:off_xbc].astype(BF16)
    wxbc = w_in0[:, off_xbc:off_dt].astype(BF16)
    wdt_t = w_in0[:, off_dt:off_gate].T.astype(BF16)
    wgate = w_in0[:, off_gate:].astype(BF16)
    dt_bias_col = ssm_dt_bias[0].reshape(2 * heads, 1).astype(F32)
    a_col = (-jnp.exp(ssm_a_log[0].astype(F32))).reshape(2 * heads, 1)
    dskip = jnp.repeat(ssm_d[0].astype(F32), SSM_HEAD_DIM).reshape(1, inner)

    u3, h0 = _prep(x, meta_tokens, w_norm_mix[0])
    u2 = u3.reshape(r, d)
    c2 = _glu(u2, wa, wg)
    zs = _mm(u2, wz, "silu")
    xbc = _mm(u2, wxbc, None)
    gates = _mm(u2, wgate, "sigmoid")
    dtt = _dt_proj(u3, wdt_t, dt_bias_col, pad)

    gc = _conf_branch(c2.reshape(b, tp, cw), w_conf_dw[0], b_conf_dw[0], conf_ln_g[0], conf_ln_b[0],
                      w_conf_out[0].astype(BF16), gates.reshape(b, tp, 2 * d))

    xs, xst, bm, cm = _ssm_conv(xbc.reshape(b, tp, xw), w_ssm_conv[0], b_ssm_conv[0], pad, inner)
    yf = _ssd(xst, dtt, bm, cm, a_col, 0, heads)
    yb = _ssd(xst, dtt, bm, cm, a_col, 1, heads)

    h1, aff_t = _merge(yf.reshape(r, inner), yb.reshape(r, inner), xs.reshape(r, inner), zs, dskip,
                       w_ssm_norm[0].reshape(1, inner).astype(F32), w_ssm_out[0].astype(BF16),
                       gc.reshape(r, d), gates, w_out[0].astype(BF16), h0.reshape(r, d),
                       w_norm_ffn[0].reshape(1, d).astype(F32), w_router[0].T.astype(BF16))

    idx, wts = _select(aff_t, b, tp, pad, cap, capp)
    idx_flat = idx.reshape(b * n_exp * capp)
    h1_3 = h1.reshape(b, tp, d)
    xn = _gather(idx_flat, h1_3, w_norm_ffn[0].reshape(1, d).astype(F32), n_exp, capp)
    y = _ffn(xn, wts, w_exp_gate[0], w_exp_up[0], w_exp_down[0])
    delta = _scatter(idx_flat, y, tp)
    return _final(h1_3, delta, w_norm_final.reshape(1, d).astype(F32))
```

```python
import functools

import jax
import jax.numpy as jnp
from jax import lax
from jax.experimental import pallas as pl
from jax.experimental.pallas import tpu as pltpu

CHUNK = 128
SSM_STATE = 128
SSM_HEAD_DIM = 64
SSM_GROUPS = 4
CAPACITY_FACTOR = 2
EPS = 1e-6
V7X_VMEM_LIMIT = 52 * 1024 * 1024
F32 = jnp.float32
BF16 = jnp.bfloat16

_NT = (((1,), (1,)), ((), ()))


def _params(*sem):
    return pltpu.CompilerParams(dimension_semantics=sem, vmem_limit_bytes=V7X_VMEM_LIMIT)


def _tile(n, cap, mult=128):
    best = None
    for t in range(mult, min(n, cap) + 1, mult):
        if n % t == 0:
            best = t
    assert best is not None, (n, cap, mult)
    return best


def _sigmoid(v):
    return 0.5 * jnp.tanh(0.5 * v) + 0.5


def _prep_kernel(x_ref, meta_ref, w_ref, u_ref, h_ref, *, n_meta):
    c = pl.program_id(1)
    w = w_ref[...]

    def emit(rows):
        h_ref[0] = rows
        ms = jnp.mean(rows * rows, axis=-1, keepdims=True)
        u_ref[0] = (rows * lax.rsqrt(ms + EPS) * w).astype(u_ref.dtype)

    @pl.when(c == 0)
    def _():
        d = meta_ref.shape[1]
        emit(jnp.concatenate([jnp.zeros((CHUNK - n_meta, d), F32), meta_ref[...]], axis=0))

    @pl.when(c > 0)
    def _():
        emit(x_ref[0])


def _prep(x, meta, w_norm):
    b, seq, d = x.shape
    n_meta = meta.shape[0]
    nch = seq // CHUNK + 1
    tp = nch * CHUNK
    return pl.pallas_call(
        functools.partial(_prep_kernel, n_meta=n_meta),
        grid=(b, nch),
        in_specs=[
            pl.BlockSpec((1, CHUNK, d), lambda i, c: (i, jnp.maximum(c - 1, 0), 0)),
            pl.BlockSpec((n_meta, d), lambda i, c: (0, 0)),
            pl.BlockSpec((1, d), lambda i, c: (0, 0)),
        ],
        out_specs=[
            pl.BlockSpec((1, CHUNK, d), lambda i, c: (i, c, 0)),
            pl.BlockSpec((1, CHUNK, d), lambda i, c: (i, c, 0)),
        ],
        out_shape=[jax.ShapeDtypeStruct((b, tp, d), BF16), jax.ShapeDtypeStruct((b, tp, d), F32)],
        compiler_params=_params("parallel", "arbitrary"),
        name="prep",
    )(x, meta, w_norm.reshape(1, d))


def _mm_kernel(u_ref, w_ref, o_ref, *, act):
    acc = jnp.dot(u_ref[...], w_ref[...], preferred_element_type=F32)
    if act == "silu":
        acc = acc * _sigmoid(acc)
    elif act == "sigmoid":
        acc = _sigmoid(acc)
    o_ref[...] = acc.astype(o_ref.dtype)


def _mm(u2, w, act, out_dtype=F32, tn=1024):
    r, k = u2.shape
    n = w.shape[1]
    tm = _tile(r, 1024)
    tn = _tile(n, tn)
    return pl.pallas_call(
        functools.partial(_mm_kernel, act=act),
        grid=(r // tm, n // tn),
        in_specs=[pl.BlockSpec((tm, k), lambda i, j: (i, 0)), pl.BlockSpec((k, tn), lambda i, j: (0, j))],
        out_specs=pl.BlockSpec((tm, tn), lambda i, j: (i, j)),
        out_shape=jax.ShapeDtypeStruct((r, n), out_dtype),
        compiler_params=_params("parallel", "arbitrary"),
        name="proj_" + str(act),
    )(u2, w)


def _glu_kernel(u_ref, wa_ref, wg_ref, o_ref):
    u = u_ref[...]
    a = jnp.dot(u, wa_ref[...], preferred_element_type=F32)
    g = jnp.dot(u, wg_ref[...], preferred_element_type=F32)
    o_ref[...] = a * _sigmoid(g)


def _glu(u2, wa, wg, tn=512):
    r, k = u2.shape
    n = wa.shape[1]
    tm = _tile(r, 1024)
    tn = _tile(n, tn)
    return pl.pallas_call(
        _glu_kernel,
        grid=(r // tm, n // tn),
        in_specs=[
            pl.BlockSpec((tm, k), lambda i, j: (i, 0)),
            pl.BlockSpec((k, tn), lambda i, j: (0, j)),
            pl.BlockSpec((k, tn), lambda i, j: (0, j)),
        ],
        out_specs=pl.BlockSpec((tm, tn), lambda i, j: (i, j)),
        out_shape=jax.ShapeDtypeStruct((r, n), F32),
        compiler_params=_params("parallel", "arbitrary"),
        name="proj_glu",
    )(u2, wa, wg)


def _dt_kernel(u_ref, wt_ref, bias_ref, o_ref, *, pad):
    raw = lax.dot_general(wt_ref[...], u_ref[0], _NT, preferred_element_type=F32)
    v = raw + bias_ref[...]
    sp = jnp.maximum(v, 0.0) + jnp.log1p(jnp.exp(-jnp.abs(v)))
    t = lax.broadcasted_iota(jnp.int32, sp.shape, 1)
    o_ref[0] = jnp.where(t >= pad, sp, 0.0)


def _dt_proj(u3, w_dt_t, bias_col, pad):
    b, tp, d = u3.shape
    h2 = w_dt_t.shape[0]
    return pl.pallas_call(
        functools.partial(_dt_kernel, pad=pad),
        grid=(b,),
        in_specs=[
            pl.BlockSpec((1, tp, d), lambda i: (i, 0, 0)),
            pl.BlockSpec((h2, d), lambda i: (0, 0)),
            pl.BlockSpec((h2, 1), lambda i: (0, 0)),
        ],
        out_specs=pl.BlockSpec((1, h2, tp), lambda i: (i, 0, 0)),
        out_shape=jax.ShapeDtypeStruct((b, h2, tp), F32),
        compiler_params=_params("parallel"),
        name="proj_dt",
    )(u3, w_dt_t, bias_col)


_CONF_HALO = 16
LANES = 128
_ROW_STRIDE = 4


def _conf_kernel(cur_ref, prev_ref, next_ref, wdw_ref, bdw_ref, g_ref, b_ref, wout_ref, gate_ref, o_ref,
                 win_ref, conv_ref, *, ktaps):
    i = pl.program_id(1)
    n_i = pl.num_programs(1)
    tm, cw = cur_ref.shape[1], cur_ref.shape[2]
    hl = _CONF_HALO
    half = (ktaps - 1) // 2
    ns = cw // LANES
    for s in range(ns):
        ls = slice(s * LANES, (s + 1) * LANES)
        win_ref[s, 0:hl, :] = jnp.where(i > 0, prev_ref[0, :, ls], 0.0)
        win_ref[s, hl:hl + tm, :] = cur_ref[0, :, ls]
        win_ref[s, hl + tm:hl + tm + hl, :] = jnp.where(i < n_i - 1, next_ref[0, :, ls], 0.0)

    blk = 8 * _ROW_STRIDE

    def row_block(rb, carry):
        base = rb * blk
        for s in range(ns):
            ls = slice(s * LANES, (s + 1) * LANES)
            for j in range(_ROW_STRIDE):
                acc = jnp.broadcast_to(bdw_ref[:, ls], (8, LANES))
                for k in range(ktaps):
                    rows = win_ref[s, pl.ds(base + j + hl - half + k, 8, stride=_ROW_STRIDE), :]
                    acc = acc + rows * wdw_ref[k:k + 1, ls]
                conv_ref[s, pl.ds(base + j, 8, stride=_ROW_STRIDE), :] = acc
        return carry

    lax.fori_loop(0, tm // blk, row_block, 0)
    cv = jnp.concatenate([conv_ref[s] for s in range(ns)], axis=1)
    mu = jnp.mean(cv, axis=-1, keepdims=True)
    xc = cv - mu
    var = jnp.mean(xc * xc, axis=-1, keepdims=True)
    y = xc * lax.rsqrt(var + EPS) * g_ref[...] + b_ref[...]
    y = y * _sigmoid(y)
    br = jnp.dot(y.astype(BF16), wout_ref[...], preferred_element_type=F32)
    o_ref[0] = gate_ref[0] * br


def _conf_branch(c3, w_dw, b_dw, ln_g, ln_b, w_out_bf, gates3):
    b, tp, cw = c3.shape
    d = w_out_bf.shape[1]
    ktaps = w_dw.shape[0]
    assert (ktaps - 1) // 2 <= _CONF_HALO
    tm = _tile(tp, 256)
    hl = _CONF_HALO
    nh = tm // hl
    n_i = tp // tm
    assert cw % LANES == 0 and tm % (8 * _ROW_STRIDE) == 0
    return pl.pallas_call(
        functools.partial(_conf_kernel, ktaps=ktaps),
        grid=(b, n_i),
        in_specs=[
            pl.BlockSpec((1, tm, cw), lambda bi, i: (bi, i, 0)),
            pl.BlockSpec((1, hl, cw), lambda bi, i: (bi, jnp.maximum(i * nh - 1, 0), 0)),
            pl.BlockSpec((1, hl, cw), lambda bi, i: (bi, jnp.minimum((i + 1) * nh, tp // hl - 1), 0)),
            pl.BlockSpec((ktaps, cw), lambda bi, i: (0, 0)),
            pl.BlockSpec((1, cw), lambda bi, i: (0, 0)),
            pl.BlockSpec((1, cw), lambda bi, i: (0, 0)),
            pl.BlockSpec((1, cw), lambda bi, i: (0, 0)),
            pl.BlockSpec((cw, d), lambda bi, i: (0, 0)),
            pl.BlockSpec((1, tm, d), lambda bi, i: (bi, i, 0)),
        ],
        out_specs=pl.BlockSpec((1, tm, d), lambda bi, i: (bi, i, 0)),
        out_shape=jax.ShapeDtypeStruct((b, tp, d), F32),
        scratch_shapes=[
            pltpu.VMEM((cw // LANES, tm + 2 * hl, LANES), F32),
            pltpu.VMEM((cw // LANES, tm, LANES), F32),
        ],
        compiler_params=_params("parallel", "arbitrary"),
        name="conf_branch",
    )(c3, c3, c3, w_dw, b_dw.reshape(1, cw), ln_g.reshape(1, cw), ln_b.reshape(1, cw), w_out_bf, gates3)


_SSM_HALO = 8


def _ssm_conv_kernel(cur_ref, prev_ref, next_ref, w_ref, b_ref, xs_ref, xst_ref, bm_ref, cm_ref,
                     win_ref, conv_ref, *, ktaps, pad, inner):
    c = pl.program_id(1)
    n_c = pl.num_programs(1)
    tm, xw = cur_ref.shape[1], cur_ref.shape[2]
    hl = _SSM_HALO
    half = (ktaps - 1) // 2
    ns = xw // LANES
    for s in range(ns):
        ls = slice(s * LANES, (s + 1) * LANES)
        win_ref[s, 0:hl, :] = jnp.where(c > 0, prev_ref[0, :, ls], 0.0)
        win_ref[s, hl:hl + tm, :] = cur_ref[0, :, ls]
        win_ref[s, hl + tm:hl + tm + hl, :] = jnp.where(c < n_c - 1, next_ref[0, :, ls], 0.0)
    blk = 8 * _ROW_STRIDE
    row_in_vreg = _ROW_STRIDE * lax.broadcasted_iota(jnp.int32, (8, 1), 0)
    for s in range(ns):
        ls = slice(s * LANES, (s + 1) * LANES)
        taps = [jnp.broadcast_to(w_ref[k:k + 1, ls], (8, LANES)) for k in range(ktaps)]
        bias = jnp.broadcast_to(b_ref[:, ls], (8, LANES))
        for rb in range(tm // blk):
            for j in range(_ROW_STRIDE):
                r0 = rb * blk + j
                acc = bias
                for k in range(ktaps):
                    acc = acc + win_ref[s, pl.ds(r0 + hl - half + k, 8, stride=_ROW_STRIDE), :] * taps[k]
                acc = acc * _sigmoid(acc)
                valid = (c * tm + r0 + row_in_vreg) >= pad
                conv_ref[s, pl.ds(r0, 8, stride=_ROW_STRIDE), :] = jnp.where(valid, acc, 0.0)
    gn = (xw - inner) // 2
    for s in range(ns):
        ls = slice(s * LANES, (s + 1) * LANES)
        slab = conv_ref[s]
        if s * LANES < inner:
            xs_ref[0, :, ls] = slab.astype(BF16)
            xst_ref[0, ls, :] = slab.T.astype(BF16)
        elif s * LANES < inner + gn:
            bm_ref[0, :, s * LANES - inner:(s + 1) * LANES - inner] = slab.astype(BF16)
        else:
            cm_ref[0, :, s * LANES - inner - gn:(s + 1) * LANES - inner - gn] = slab.astype(BF16)


def _ssm_conv(xbc3, w, bias, pad, inner):
    b, tp, xw = xbc3.shape
    ktaps = w.shape[0]
    tm = CHUNK
    hl = _SSM_HALO
    nh = tm // hl
    gn = (xw - inner) // 2
    assert inner % LANES == 0 and gn % LANES == 0 and (ktaps - 1) // 2 <= hl
    return pl.pallas_call(
        functools.partial(_ssm_conv_kernel, ktaps=ktaps, pad=pad, inner=inner),
        grid=(b, tp // tm),
        in_specs=[
            pl.BlockSpec((1, tm, xw), lambda bi, i: (bi, i, 0)),
            pl.BlockSpec((1, hl, xw), lambda bi, i: (bi, jnp.maximum(i * nh - 1, 0), 0)),
            pl.BlockSpec((1, hl, xw), lambda bi, i: (bi, jnp.minimum((i + 1) * nh, tp // hl - 1), 0)),
            pl.BlockSpec((ktaps, xw), lambda bi, i: (0, 0)),
            pl.BlockSpec((1, xw), lambda bi, i: (0, 0)),
        ],
        out_specs=[
            pl.BlockSpec((1, tm, inner), lambda bi, i: (bi, i, 0)),
            pl.BlockSpec((1, inner, tm), lambda bi, i: (bi, 0, i)),
            pl.BlockSpec((1, tm, gn), lambda bi, i: (bi, i, 0)),
            pl.BlockSpec((1, tm, gn), lambda bi, i: (bi, i, 0)),
        ],
        out_shape=[
            jax.ShapeDtypeStruct((b, tp, inner), BF16),
            jax.ShapeDtypeStruct((b, inner, tp), BF16),
            jax.ShapeDtypeStruct((b, tp, gn), BF16),
            jax.ShapeDtypeStruct((b, tp, gn), BF16),
        ],
        scratch_shapes=[pltpu.VMEM((xw // LANES, tm + 2 * hl, LANES), F32), pltpu.VMEM((xw // LANES, tm, LANES), F32)],
        compiler_params=_params("parallel", "arbitrary"),
        name="ssm_conv",
    )(xbc3, xbc3, xbc3, w, bias.reshape(1, xw))


def _ssd_kernel(xst_ref, dtt_ref, bm_ref, cm_ref, a_ref, y_ref, s_ref, *, reverse, heads, groups):
    c = pl.program_id(1)
    p, n = SSM_HEAD_DIM, SSM_STATE
    hpg = heads // groups

    @pl.when(c == 0)
    def _():
        s_ref[...] = jnp.zeros_like(s_ref)

    dtt = dtt_ref[0]
    at = dtt * a_ref[...]
    li = lax.broadcasted_iota(jnp.int32, (CHUNK, CHUNK), 0)
    ri = lax.broadcasted_iota(jnp.int32, (CHUNK, CHUNK), 1)
    keep = (ri >= li) if reverse else (ri <= li)
    mx = keep.astype(F32)
    hi = lax.Precision.HIGHEST
    cum = lax.dot_general(mx, at, _NT, precision=hi, preferred_element_type=F32)
    cumt = lax.dot_general(at, mx, _NT, precision=hi, preferred_element_type=F32)
    edge = 0 if reverse else CHUNK - 1
    tott = cumt[:, edge:edge + 1]
    wt = dtt * jnp.exp(tott - cumt)
    etot = jnp.exp(tott)
    assert n == CHUNK

    for g in range(groups):
        cg = cm_ref[0, :, g * n:(g + 1) * n]
        bg = bm_ref[0, :, g * n:(g + 1) * n]
        cb = lax.dot_general(cg, bg, _NT, preferred_element_type=F32)
        cg32 = cg.astype(F32)
        for j in range(hpg):
            h = g * hpg + j
            col = jnp.broadcast_to(cum[:, h:h + 1], (CHUNK, CHUNK))
            decay = jnp.exp(jnp.where(keep, col - cumt[h:h + 1, :], -jnp.inf))
            m = (cb * decay).astype(BF16)
            ce = (cg32 * jnp.exp(col)).astype(BF16)
            lhs = jnp.concatenate([m, ce], axis=1)
            xt = xst_ref[0, h * p:(h + 1) * p, :]
            sh = s_ref[h]
            rhs_t = jnp.concatenate([(xt * dtt[h:h + 1, :]).astype(BF16), sh.astype(BF16)], axis=1)
            y_ref[0, :, h * p:(h + 1) * p] = lax.dot_general(
                lhs, rhs_t, _NT, preferred_element_type=F32).astype(y_ref.dtype)
            xw = (xt * wt[h:h + 1, :]).astype(BF16)
            s_ref[h] = sh * etot[h:h + 1, :] + jnp.dot(xw, bg, preferred_element_type=F32)


def _ssd(xst, dtt_all, bm, cm, a_col_all, direction, heads):
    b, hp, tp = xst.shape
    nch = tp // CHUNK
    gn = bm.shape[2]
    reverse = direction == 1
    cidx = (lambda c: nch - 1 - c) if reverse else (lambda c: c)
    return pl.pallas_call(
        functools.partial(_ssd_kernel, reverse=reverse, heads=heads, groups=SSM_GROUPS),
        grid=(b, nch),
        in_specs=[
            pl.BlockSpec((1, hp, CHUNK), lambda i, c: (i, 0, cidx(c))),
            pl.BlockSpec((1, heads, CHUNK), lambda i, c: (i, direction, cidx(c))),
            pl.BlockSpec((1, CHUNK, gn), lambda i, c: (i, cidx(c), 0)),
            pl.BlockSpec((1, CHUNK, gn), lambda i, c: (i, cidx(c), 0)),
            pl.BlockSpec((heads, 1), lambda i, c: (direction, 0)),
        ],
        out_specs=pl.BlockSpec((1, CHUNK, hp), lambda i, c: (i, cidx(c), 0)),
        out_shape=jax.ShapeDtypeStruct((b, tp, hp), BF16),
        scratch_shapes=[pltpu.VMEM((heads, SSM_HEAD_DIM, SSM_STATE), F32)],
        compiler_params=_params("parallel", "arbitrary"),
        name="ssd_rev" if reverse else "ssd_fwd",
    )(xst, dtt_all, bm, cm, a_col_all)


def _merge_kernel(yf_ref, yb_ref, xs_ref, zs_ref, dsk_ref, wsn_ref, wso_ref, gc_ref, g1_ref, wo_ref, h0_ref,
                  wnf_ref, wrt_ref, h1_ref, aff_ref):
    y = yf_ref[...].astype(F32) + yb_ref[...].astype(F32) + dsk_ref[...] * xs_ref[...].astype(F32)
    y = y * zs_ref[...]
    ms = jnp.mean(y * y, axis=-1, keepdims=True)
    yn = (y * lax.rsqrt(ms + EPS) * wsn_ref[...]).astype(BF16)
    bs = jnp.dot(yn, wso_ref[...], preferred_element_type=F32)
    merged = gc_ref[...] + g1_ref[...] * bs
    h1 = h0_ref[...] + jnp.dot(merged.astype(BF16), wo_ref[...], preferred_element_type=F32)
    h1_ref[...] = h1
    ms1 = jnp.mean(h1 * h1, axis=-1, keepdims=True)
    hn = (h1 * lax.rsqrt(ms1 + EPS) * wnf_ref[...]).astype(BF16)
    logits = lax.dot_general(wrt_ref[...], hn, _NT, preferred_element_type=F32)
    mx = jnp.max(logits, axis=0, keepdims=True)
    ex = jnp.exp(logits - mx)
    aff_ref[...] = ex / jnp.sum(ex, axis=0, keepdims=True)


def _merge(yf, yb, xs, zs, dskip, w_ssm_norm, w_ssm_out_bf, gc, gates, w_out_bf, h0, w_norm_ffn, w_router_t_bf):
    r, inner = yf.shape
    d = h0.shape[1]
    e = w_router_t_bf.shape[0]
    tm = _tile(r, 256)
    row = lambda i: (i, 0)
    fix = lambda i: (0, 0)
    return pl.pallas_call(
        _merge_kernel,
        grid=(r // tm,),
        in_specs=[
            pl.BlockSpec((tm, inner), row), pl.BlockSpec((tm, inner), row), pl.BlockSpec((tm, inner), row),
            pl.BlockSpec((tm, inner), row), pl.BlockSpec((1, inner), fix), pl.BlockSpec((1, inner), fix),
            pl.BlockSpec((inner, d), fix), pl.BlockSpec((tm, d), row), pl.BlockSpec((tm, d), lambda i: (i, 1)),
            pl.BlockSpec((d, d), fix), pl.BlockSpec((tm, d), row), pl.BlockSpec((1, d), fix),
            pl.BlockSpec((e, d), fix),
        ],
        out_specs=[pl.BlockSpec((tm, d), row), pl.BlockSpec((e, tm), lambda i: (0, i))],
        out_shape=[jax.ShapeDtypeStruct((r, d), F32), jax.ShapeDtypeStruct((e, r), F32)],
        compiler_params=_params("parallel"),
        name="merge_router",
    )(yf, yb, xs, zs, dskip, w_ssm_norm, w_ssm_out_bf, gc, gates, w_out_bf, h0, w_norm_ffn, w_router_t_bf)


def _select_kernel(aff_ref, idx_ref, wts_ref, rank_ref, *, pad, cap):
    e, tp = aff_ref.shape
    n_hi, n_lo = idx_ref.shape[2], idx_ref.shape[3]
    aff = aff_ref[...]
    tpos = lax.broadcasted_iota(jnp.int32, (e, tp), 1)
    bits = jnp.where(tpos >= pad, pltpu.bitcast(aff, jnp.int32), -1)

    def refine(i, thr):
        cand = thr | lax.shift_left(jnp.int32(1), 30 - i)
        cnt = jnp.sum((bits >= cand).astype(jnp.int32), axis=1, keepdims=True)
        return jnp.where(cnt >= cap, cand, thr)

    thr = lax.fori_loop(0, 31, refine, jnp.zeros((e, 1), jnp.int32))
    gt = bits > thr
    eq = bits == thr
    need = cap - jnp.sum(gt.astype(jnp.int32), axis=1, keepdims=True)

    ri = lax.broadcasted_iota(jnp.int32, (CHUNK, CHUNK), 0)
    ci = lax.broadcasted_iota(jnp.int32, (CHUNK, CHUNK), 1)
    upper = (ri <= ci).astype(BF16)

    def prefix(flags):
        off = jnp.zeros((e, 1), F32)
        for k in range(tp // CHUNK):
            blk = flags[:, k * CHUNK:(k + 1) * CHUNK].astype(BF16)
            inc = jnp.dot(blk, upper, preferred_element_type=F32) + off
            rank_ref[:, k * CHUNK:(k + 1) * CHUNK] = inc
            off = inc[:, CHUNK - 1:CHUNK]

    eqf = jnp.where(eq, 1.0, 0.0)
    prefix(eqf)
    sel = gt | (eq & ((rank_ref[...] - eqf) < need.astype(F32)))
    prefix(jnp.where(sel, 1.0, 0.0))
    rank_ref[...] = jnp.where(sel, rank_ref[...], 0.0) - 1.0

    t1 = lax.broadcasted_iota(jnp.int32, (1, tp), 1)
    t_hi = lax.shift_right_logical(t1, 6).astype(F32)
    t_lo = (t1 & 63).astype(F32)
    hh = lax.broadcasted_iota(jnp.int32, (n_hi, 1), 0).astype(F32)
    ll = lax.broadcasted_iota(jnp.int32, (n_lo, 1), 0).astype(F32)
    jj = lax.broadcasted_iota(jnp.int32, (n_hi, n_lo), 0) * n_lo + lax.broadcasted_iota(jnp.int32, (n_hi, n_lo), 1)

    def emit(ei, carry):
        s0 = rank_ref[pl.ds(ei, 1), :]
        a0 = aff_ref[pl.ds(ei, 1), :]
        hi = jnp.floor(s0 * (1.0 / n_lo))
        lo = s0 - n_lo * hi
        a1 = a0.astype(BF16).astype(F32)
        a2 = (a0 - a1).astype(BF16).astype(F32)
        a3 = (a0 - a1 - a2).astype(BF16).astype(F32)
        in_hi = hi == hh
        parts = [jnp.where(in_hi, v, 0.0) for v in (t_hi, t_lo, a1, a2, a3)]
        parts.append(jnp.zeros((n_hi, tp), F32))
        lhs = jnp.concatenate(parts, axis=0).astype(BF16)
        rhs = jnp.where(lo == ll, 1.0, 0.0).astype(BF16)
        res = lax.dot_general(lhs, rhs, _NT, preferred_element_type=F32)
        tok = (64.0 * res[0:n_hi] + res[n_hi:2 * n_hi]).astype(jnp.int32)
        idx_ref[0, ei] = jnp.where(jj >= cap, jj - cap, tok)
        wts_ref[0, ei] = res[2 * n_hi:3 * n_hi] + res[3 * n_hi:4 * n_hi] + res[4 * n_hi:5 * n_hi]
        return carry

    lax.fori_loop(0, e, emit, 0)


_SEL_HI, _SEL_LO = 8, 128


def _select(aff_t, b, tp, pad, cap):
    e = aff_t.shape[0]
    assert cap <= _SEL_HI * _SEL_LO and tp <= 64 * 256
    return pl.pallas_call(
        functools.partial(_select_kernel, pad=pad, cap=cap),
        grid=(b,),
        in_specs=[pl.BlockSpec((e, tp), lambda i: (0, i))],
        out_specs=[
            pl.BlockSpec((1, e, _SEL_HI, _SEL_LO), lambda i: (i, 0, 0, 0)),
            pl.BlockSpec((1, e, _SEL_HI, _SEL_LO), lambda i: (i, 0, 0, 0)),
        ],
        out_shape=[jax.ShapeDtypeStruct((b, e, _SEL_HI, _SEL_LO), jnp.int32),
                   jax.ShapeDtypeStruct((b, e, _SEL_HI, _SEL_LO), F32)],
        scratch_shapes=[pltpu.VMEM((e, tp), F32)],
        compiler_params=_params("parallel"),
        name="select",
    )(aff_t)


def _gather_kernel(idx_ref, h1_ref, wn_ref, xn_ref, rows_ref, *, n_exp):
    bi, ei = pl.program_id(0), pl.program_id(1)
    capp = rows_ref.shape[0]
    base = (bi * n_exp + ei) * capp

    def copy(j, carry):
        t = idx_ref[base + j]
        rows_ref[pl.ds(j, 1), :] = h1_ref[0, pl.ds(t, 1), :]
        return carry

    lax.fori_loop(0, capp, copy, 0, unroll=8)
    rows = rows_ref[...]
    ms = jnp.mean(rows * rows, axis=-1, keepdims=True)
    xn_ref[0, 0] = (rows * lax.rsqrt(ms + EPS) * wn_ref[...]).astype(BF16)


def _gather(idx_flat, h1_3, w_norm, n_exp, capp):
    b, tp, d = h1_3.shape
    return pl.pallas_call(
        functools.partial(_gather_kernel, n_exp=n_exp),
        grid_spec=pltpu.PrefetchScalarGridSpec(
            num_scalar_prefetch=1,
            grid=(b, n_exp),
            in_specs=[
                pl.BlockSpec((1, tp, d), lambda i, e, idx: (i, 0, 0)),
                pl.BlockSpec((1, d), lambda i, e, idx: (0, 0)),
            ],
            out_specs=pl.BlockSpec((1, 1, capp, d), lambda i, e, idx: (i, e, 0, 0)),
            scratch_shapes=[pltpu.VMEM((capp, d), F32)],
        ),
        out_shape=jax.ShapeDtypeStruct((b, n_exp, capp, d), BF16),
        compiler_params=_params("parallel", "arbitrary"),
        name="moe_gather",
    )(idx_flat, h1_3, w_norm)


def _ffn_kernel(xn_ref, wts_ref, wg_ref, wu_ref, wd_ref, y_ref):
    f = pl.program_id(2)
    nb, capp, d = xn_ref.shape[0], xn_ref.shape[2], xn_ref.shape[3]
    xn = xn_ref[...].reshape(nb * capp, d)
    hg = jnp.dot(xn, wg_ref[0].astype(BF16), preferred_element_type=F32)
    hu = jnp.dot(xn, wu_ref[0].astype(BF16), preferred_element_type=F32)
    act = (hg * _sigmoid(hg) * hu).astype(BF16)
    yp = jnp.dot(act, wd_ref[0].astype(BF16), preferred_element_type=F32).reshape(nb, 1, capp, d)

    @pl.when(f == 0)
    def _():
        y_ref[...] = yp

    @pl.when(f > 0)
    def _():
        y_ref[...] += yp

    @pl.when(f == pl.num_programs(2) - 1)
    def _():
        y_ref[...] = y_ref[...] * wts_ref[...]


def _ffn(xn, wts, w_gate, w_up, w_down):
    b, n_exp, capp, d = xn.shape
    ff = w_gate.shape[2]
    nb = 2 if b % 2 == 0 else 1
    tf = _tile(ff, 512)
    return pl.pallas_call(
        _ffn_kernel,
        grid=(n_exp, b // nb, ff // tf),
        in_specs=[
            pl.BlockSpec((nb, 1, capp, d), lambda e, m, f: (m, e, 0, 0)),
            pl.BlockSpec((nb, 1, capp, 1), lambda e, m, f: (m, e, 0, 0)),
            pl.BlockSpec((1, d, tf), lambda e, m, f: (e, 0, f)),
            pl.BlockSpec((1, d, tf), lambda e, m, f: (e, 0, f)),
            pl.BlockSpec((1, tf, d), lambda e, m, f: (e, f, 0)),
        ],
        out_specs=pl.BlockSpec((nb, 1, capp, d), lambda e, m, f: (m, e, 0, 0)),
        out_shape=jax.ShapeDtypeStruct((b, n_exp, capp, d), F32),
        compiler_params=_params("parallel", "parallel", "arbitrary"),
        name="moe_ffn",
    )(xn, wts, w_gate, w_up, w_down)


_SCATTER_GROUP = 8


def _scatter_kernel(idx_ref, y_ref, o_ref, *, n_exp):
    bi, ei = pl.program_id(0), pl.program_id(1)
    capp = y_ref.shape[2]
    base = (bi * n_exp + ei) * capp

    @pl.when(ei == 0)
    def _():
        o_ref[...] = jnp.zeros_like(o_ref)

    def add_group(g, carry):
        j0 = pl.multiple_of(g * _SCATTER_GROUP, _SCATTER_GROUP)
        toks = [idx_ref[base + j0 + k] for k in range(_SCATTER_GROUP)]
        rows = [o_ref[0, pl.ds(t, 1), :] for t in toks]
        ytile = y_ref[0, 0, pl.ds(j0, _SCATTER_GROUP), :]
        for k, t in enumerate(toks):
            o_ref[0, pl.ds(t, 1), :] = rows[k] + ytile[k:k + 1, :]
        return carry

    lax.fori_loop(0, capp // _SCATTER_GROUP, add_group, 0)


def _scatter(idx_flat, y, tp):
    b, n_exp, capp, d = y.shape
    return pl.pallas_call(
        functools.partial(_scatter_kernel, n_exp=n_exp),
        grid_spec=pltpu.PrefetchScalarGridSpec(
            num_scalar_prefetch=1,
            grid=(b, n_exp),
            in_specs=[pl.BlockSpec((1, 1, capp, d), lambda i, e, idx: (i, e, 0, 0))],
            out_specs=pl.BlockSpec((1, tp, d), lambda i, e, idx: (i, 0, 0)),
        ),
        out_shape=jax.ShapeDtypeStruct((b, tp, d), F32),
        compiler_params=_params("parallel", "arbitrary"),
        name="moe_scatter",
    )(idx_flat, y)


def _final_kernel(h1_ref, dl_ref, w_ref, o_ref):
    h = h1_ref[0] + dl_ref[0]
    ms = jnp.mean(h * h, axis=-1, keepdims=True)
    o_ref[0] = h * lax.rsqrt(ms + EPS) * w_ref[...]


def _final(h1_3, delta, w_norm):
    b, tp, d = h1_3.shape
    seq = tp - CHUNK
    return pl.pallas_call(
        _final_kernel,
        grid=(b, seq // CHUNK),
        in_specs=[
            pl.BlockSpec((1, CHUNK, d), lambda i, c: (i, c + 1, 0)),
            pl.BlockSpec((1, CHUNK, d), lambda i, c: (i, c + 1, 0)),
            pl.BlockSpec((1, d), lambda i, c: (0, 0)),
        ],
        out_specs=pl.BlockSpec((1, CHUNK, d), lambda i, c: (i, c, 0)),
        out_shape=jax.ShapeDtypeStruct((b, seq, d), F32),
        compiler_params=_params("parallel", "parallel"),
        name="final_norm",
    )(h1_3, delta, w_norm)


def kernel(x, meta_tokens, w_norm_mix, w_in, w_conf_dw, b_conf_dw, conf_ln_g, conf_ln_b, w_conf_out,
           w_ssm_conv, b_ssm_conv, ssm_dt_bias, ssm_a_log, ssm_d, w_ssm_norm, w_ssm_out, w_out,
           w_norm_ffn, w_router, w_exp_gate, w_exp_up, w_exp_down, w_norm_final):
    b, seq, d = x.shape
    depth = w_in.shape[0]
    assert depth == 1 and seq % CHUNK == 0
    n_meta = meta_tokens.shape[0]
    pad = CHUNK - n_meta
    lt = n_meta + seq
    tp = seq + CHUNK
    r = b * tp
    cw = w_conf_dw.shape[2]
    heads = ssm_d.shape[1]
    inner = heads * SSM_HEAD_DIM
    xw = w_ssm_conv.shape[2]
    n_exp = w_router.shape[2]
    cap = CAPACITY_FACTOR * lt // n_exp
    capp = -(-cap // 8) * 8
    off_z = 2 * cw
    off_xbc = off_z + inner
    off_dt = off_xbc + xw
    off_gate = off_dt + 2 * heads

    w_in0 = w_in[0]
    wa = w_in0[:, 0:cw].astype(BF16)
    wg = w_in0[:, cw:off_z].astype(BF16)
    wz = w_in0[:, off_z:off_xbc].astype(BF16)
    wxbc = w_in0[:, off_xbc:off_dt].astype(BF16)
    wdt_t = w_in0[:, off_dt:off_gate].T.astype(BF16)
    wgate = w_in0[:, off_gate:].astype(BF16)
    dt_bias_col = ssm_dt_bias[0].reshape(2 * heads, 1).astype(F32)
    a_col = (-jnp.exp(ssm_a_log[0].astype(F32))).reshape(2 * heads, 1)
    dskip = jnp.repeat(ssm_d[0].astype(F32), SSM_HEAD_DIM).reshape(1, inner)

    u3, h0 = _prep(x, meta_tokens, w_norm_mix[0])
    u2 = u3.reshape(r, d)
    c2 = _glu(u2, wa, wg)
    zs = _mm(u2, wz, "silu", out_dtype=BF16)
    xbc = _mm(u2, wxbc, None)
    gates = _mm(u2, wgate, "sigmoid", out_dtype=BF16)
    dtt = _dt_proj(u3, wdt_t, dt_bias_col, pad)

    gc = _conf_branch(c2.reshape(b, tp, cw), w_conf_dw[0], b_conf_dw[0], conf_ln_g[0], conf_ln_b[0],
                      w_conf_out[0].astype(BF16), gates.reshape(b, tp, 2 * d))

    xs, xst, bm, cm = _ssm_conv(xbc.reshape(b, tp, xw), w_ssm_conv[0], b_ssm_conv[0], pad, inner)
    yf = _ssd(xst, dtt, bm, cm, a_col, 0, heads)
    yb = _ssd(xst, dtt, bm, cm, a_col, 1, heads)

    h1, aff_t = _merge(yf.reshape(r, inner), yb.reshape(r, inner), xs.reshape(r, inner), zs, dskip,
                       w_ssm_norm[0].reshape(1, inner).astype(F32), w_ssm_out[0].astype(BF16),
                       gc.reshape(r, d), gates, w_out[0].astype(BF16), h0.reshape(r, d),
                       w_norm_ffn[0].reshape(1, d).astype(F32), w_router[0].T.astype(BF16))

    idx, wts = _select(aff_t, b, tp, pad, cap)
    idx_flat = idx.reshape(b, n_exp, -1)[:, :, :capp].reshape(b * n_exp * capp)
    wts = wts.reshape(b, n_exp, -1)[:, :, :capp].reshape(b, n_exp, capp, 1)
    h1_3 = h1.reshape(b, tp, d)
    xn = _gather(idx_flat, h1_3, w_norm_ffn[0].reshape(1, d).astype(F32), n_exp, capp)
    y = _ffn(xn, wts, w_exp_gate[0], w_exp_up[0], w_exp_down[0])
    delta = _scatter(idx_flat, y, tp)
    return _final(h1_3, delta, w_norm_final.reshape(1, d).astype(F32))
```

```python
import functools

import jax
import jax.numpy as jnp
from jax import lax
from jax.experimental import pallas as pl
from jax.experimental.pallas import tpu as pltpu

CHUNK = 128
SSM_STATE = 128
SSM_HEAD_DIM = 64
SSM_GROUPS = 4
CAPACITY_FACTOR = 2
EPS = 1e-6
V7X_VMEM_LIMIT = 52 * 1024 * 1024
F32 = jnp.float32
BF16 = jnp.bfloat16

_NT = (((1,), (1,)), ((), ()))


def _params(*sem):
    return pltpu.CompilerParams(dimension_semantics=sem, vmem_limit_bytes=V7X_VMEM_LIMIT)


def _tile(n, cap, mult=128):
    best = None
    for t in range(mult, min(n, cap) + 1, mult):
        if n % t == 0:
            best = t
    assert best is not None, (n, cap, mult)
    return best


def _sigmoid(v):
    return 0.5 * jnp.tanh(0.5 * v) + 0.5


def _silu(v):
    h = 0.5 * v
    return h + h * jnp.tanh(h)


def _prep_kernel(x_ref, meta_ref, w_ref, u_ref, h_ref, *, n_meta):
    c = pl.program_id(1)
    w = w_ref[...]

    def emit(rows):
        h_ref[...] = rows
        ms = jnp.mean(rows * rows, axis=-1, keepdims=True)
        u_ref[...] = (rows * lax.rsqrt(ms + EPS) * w).astype(u_ref.dtype)

    @pl.when(c == 0)
    def _():
        tm, d = x_ref.shape
        emit(jnp.concatenate([jnp.zeros((CHUNK - n_meta, d), F32), meta_ref[...], x_ref[0:tm - CHUNK, :]], axis=0))

    @pl.when(c > 0)
    def _():
        emit(x_ref[...])


def _prep(x, meta, w_norm):
    b, seq, d = x.shape
    n_meta = meta.shape[0]
    tp = seq + CHUNK
    tm = _tile(tp, _FUSED_ROWS)
    n_t = tp // tm
    assert tm > CHUNK
    dst = pl.BlockSpec((tm, d), lambda i, c: (i * n_t + c, 0))
    return pl.pallas_call(
        functools.partial(_prep_kernel, n_meta=n_meta),
        grid=(b, n_t),
        in_specs=[
            pl.BlockSpec((pl.Element(tm), pl.Element(d)),
                         lambda i, c: (pl.multiple_of(i * seq + jnp.maximum(c * tm - CHUNK, 0), CHUNK), 0)),
            pl.BlockSpec((n_meta, d), lambda i, c: (0, 0)),
            pl.BlockSpec((1, d), lambda i, c: (0, 0)),
        ],
        out_specs=[dst, dst],
        out_shape=[jax.ShapeDtypeStruct((b * tp, d), BF16), jax.ShapeDtypeStruct((b * tp, d), F32)],
        compiler_params=_params("parallel", "arbitrary"),
        name="prep",
    )(x.reshape(b * seq, d), meta, w_norm.reshape(1, d))


def _mm_kernel(u_ref, w_ref, o_ref, *, act):
    acc = jnp.dot(u_ref[...], w_ref[...], preferred_element_type=F32)
    if act == "silu":
        acc = _silu(acc)
    elif act == "sigmoid":
        acc = _sigmoid(acc)
    o_ref[...] = acc.astype(o_ref.dtype)


def _mm(u2, w, act, out_dtype=F32, tn=1024):
    r, k = u2.shape
    n = w.shape[1]
    tm = _tile(r, 1024)
    tn = _tile(n, tn)
    return pl.pallas_call(
        functools.partial(_mm_kernel, act=act),
        grid=(r // tm, n // tn),
        in_specs=[pl.BlockSpec((tm, k), lambda i, j: (i, 0)), pl.BlockSpec((k, tn), lambda i, j: (0, j))],
        out_specs=pl.BlockSpec((tm, tn), lambda i, j: (i, j)),
        out_shape=jax.ShapeDtypeStruct((r, n), out_dtype),
        compiler_params=_params("parallel", "arbitrary"),
        name="proj_" + str(act),
    )(u2, w)


def _dt_kernel(u_ref, wt_ref, bias_ref, o_ref, *, pad):
    raw = lax.dot_general(wt_ref[...], u_ref[0], _NT, preferred_element_type=F32)
    v = raw + bias_ref[...]
    sp = jnp.maximum(v, 0.0) + jnp.log1p(jnp.exp(-jnp.abs(v)))
    t = lax.broadcasted_iota(jnp.int32, sp.shape, 1)
    o_ref[0] = jnp.where(t >= pad, sp, 0.0)


def _dt_proj(u3, w_dt_t, bias_col, pad):
    b, tp, d = u3.shape
    h2 = w_dt_t.shape[0]
    return pl.pallas_call(
        functools.partial(_dt_kernel, pad=pad),
        grid=(b,),
        in_specs=[
            pl.BlockSpec((1, tp, d), lambda i: (i, 0, 0)),
            pl.BlockSpec((h2, d), lambda i: (0, 0)),
            pl.BlockSpec((h2, 1), lambda i: (0, 0)),
        ],
        out_specs=pl.BlockSpec((1, h2, tp), lambda i: (i, 0, 0)),
        out_shape=jax.ShapeDtypeStruct((b, h2, tp), F32),
        compiler_params=_params("parallel"),
        name="proj_dt",
    )(u3, w_dt_t, bias_col)


LANES = 128
_ROW_STRIDE = 4
_U_HALO = 16
_FUSED_ROWS = 384
_PROJ_GROUP = 256


def _fill_lhs(lhs_ref, cur_ref, prev_ref, next_ref):
    i = pl.program_id(1)
    tm, hl = cur_ref.shape[1], prev_ref.shape[1]
    zero = jnp.zeros(prev_ref.shape[1:], prev_ref.dtype)
    lhs_ref[0:hl, :] = jnp.where(i > 0, prev_ref[0], zero)
    lhs_ref[hl:hl + tm, :] = cur_ref[0]
    lhs_ref[hl + tm:hl + tm + hl, :] = jnp.where(i < pl.num_programs(1) - 1, next_ref[0], zero)


def _conf_kernel(cur_ref, prev_ref, next_ref, wa_ref, wg_ref, wdw_ref, bdw_ref, g_ref, b_ref, wout_ref, wgate_ref,
                 o_ref, lhs_ref, win_ref, conv_ref, *, ktaps):
    tm = cur_ref.shape[1]
    hl = _U_HALO
    half = (ktaps - 1) // 2
    n_groups, _, gw = wa_ref.shape
    spg = gw // LANES
    ns = n_groups * spg
    _fill_lhs(lhs_ref, cur_ref, prev_ref, next_ref)
    blk = 8 * _ROW_STRIDE
    for s in range(ns):
        if s % spg == 0:
            lhs = lhs_ref[...]
            a = jnp.dot(lhs, wa_ref[s // spg], preferred_element_type=F32)
            g = jnp.dot(lhs, wg_ref[s // spg], preferred_element_type=F32)
            glu = a * _sigmoid(g)
            for q in range(spg):
                win_ref[s + q] = glu[:, q * LANES:(q + 1) * LANES]
        ls = slice(s * LANES, (s + 1) * LANES)
        bias = jnp.broadcast_to(bdw_ref[:, ls], (8, LANES))
        for rb in range(tm // blk):
            for j in range(_ROW_STRIDE):
                r0 = rb * blk + j
                acc = bias
                for k in range(ktaps):
                    rows = win_ref[s, pl.ds(r0 + hl - half + k, 8, stride=_ROW_STRIDE), :]
                    acc = acc + rows * wdw_ref[k:k + 1, ls]
                conv_ref[s, pl.ds(r0, 8, stride=_ROW_STRIDE), :] = acc
    cv = jnp.concatenate([conv_ref[s] for s in range(ns)], axis=1)
    mu = jnp.mean(cv, axis=-1, keepdims=True)
    xc = cv - mu
    var = jnp.mean(xc * xc, axis=-1, keepdims=True)
    y = xc * lax.rsqrt(var + EPS) * g_ref[...] + b_ref[...]
    y = _silu(y)
    br =jnp.dot(y.astype(BF16), wout_ref[...], preferred_element_type=F32)
    gate = _sigmoid(jnp.dot(cur_ref[0], wgate_ref[...], preferred_element_type=F32))
    o_ref[0] = gate * br


def _conf_branch(u3, wa_bf, wg_bf, w_dw, b_dw, ln_g, ln_b, w_out_bf, w_gate_bf):
    b, tp, d = u3.shape
    cw = wa_bf.shape[1]
    ktaps = w_dw.shape[0]
    hl = _U_HALO
    tm = _tile(tp, _FUSED_ROWS)
    assert (ktaps - 1) // 2 <= hl and cw % _PROJ_GROUP == 0 and tm % (8 * _ROW_STRIDE) == 0
    fix2 = lambda bi, i: (0, 0)
    return pl.pallas_call(
        functools.partial(_conf_kernel, ktaps=ktaps),
        grid=(b, tp // tm),
        in_specs=_u_specs(tm, d, tp) + [
            pl.BlockSpec((cw // _PROJ_GROUP, d, _PROJ_GROUP), lambda bi, i: (0, 0, 0)),
            pl.BlockSpec((cw // _PROJ_GROUP, d, _PROJ_GROUP), lambda bi, i: (0, 0, 0)),
            pl.BlockSpec((ktaps, cw), fix2),
            pl.BlockSpec((1, cw), fix2),
            pl.BlockSpec((1, cw), fix2),
            pl.BlockSpec((1, cw), fix2),
            pl.BlockSpec((cw, d), fix2),
            pl.BlockSpec((d, d), fix2),
        ],
        out_specs=pl.BlockSpec((1, tm, d), lambda bi, i: (bi, i, 0)),
        out_shape=jax.ShapeDtypeStruct((b, tp, d), F32),
        scratch_shapes=[
            pltpu.VMEM((tm + 2 * hl, d), BF16),
            pltpu.VMEM((cw // LANES, tm + 2 * hl, LANES), F32),
            pltpu.VMEM((cw // LANES, tm, LANES), F32),
        ],
        compiler_params=_params("parallel", "arbitrary"),
        name="conf_branch",
    )(u3, u3, u3, _group_cols(wa_bf, _PROJ_GROUP), _group_cols(wg_bf, _PROJ_GROUP), w_dw, b_dw.reshape(1, cw),
      ln_g.reshape(1, cw), ln_b.reshape(1, cw), w_out_bf, w_gate_bf)


def _ssm_conv_kernel(cur_ref, prev_ref, next_ref, wx_ref, w_ref, b_ref, xs_ref, xst_ref, bm_ref, cm_ref,
                     lhs_ref, win_ref, conv_ref, *, ktaps, pad, inner):
    c = pl.program_id(1)
    tm = cur_ref.shape[1]
    hl = _U_HALO
    half = (ktaps - 1) // 2
    n_groups, _, gw = wx_ref.shape
    spg = gw // LANES
    ns = n_groups * spg
    xw = ns * LANES
    _fill_lhs(lhs_ref, cur_ref, prev_ref, next_ref)
    blk = 8 * _ROW_STRIDE
    row_in_vreg = _ROW_STRIDE * lax.broadcasted_iota(jnp.int32, (8, 1), 0)
    for s in range(ns):
        if s % spg == 0:
            res = jnp.dot(lhs_ref[...], wx_ref[s // spg], preferred_element_type=F32)
            for q in range(spg):
                win_ref[s + q] = res[:, q * LANES:(q + 1) * LANES]
        ls = slice(s * LANES, (s + 1) * LANES)
        taps = [jnp.broadcast_to(w_ref[k:k + 1, ls], (8, LANES)) for k in range(ktaps)]
        bias = jnp.broadcast_to(b_ref[:, ls], (8, LANES))
        for rb in range(tm // blk):
            for j in range(_ROW_STRIDE):
                r0 = rb * blk + j
                acc = bias
                for k in range(ktaps):
                    acc = acc + win_ref[s, pl.ds(r0 + hl - half + k, 8, stride=_ROW_STRIDE), :] * taps[k]
                acc = _silu(acc)
                if rb * blk < pad:
                    acc = jnp.where((c * tm + r0 + row_in_vreg) >= pad, acc, 0.0)
                conv_ref[s, pl.ds(r0, 8, stride=_ROW_STRIDE), :] = acc
    gn = (xw - inner) // 2
    for s in range(ns):
        ls = slice(s * LANES, (s + 1) * LANES)
        slab = conv_ref[s]
        if s * LANES < inner:
            xs_ref[0, :, ls] = slab.astype(BF16)
            xst_ref[0, ls, :] = slab.T.astype(BF16)
        elif s * LANES < inner + gn:
            bm_ref[0, :, s * LANES - inner:(s + 1) * LANES - inner] = slab.astype(BF16)
        else:
            cm_ref[0, :, s * LANES - inner - gn:(s + 1) * LANES - inner - gn] = slab.astype(BF16)


def _group_cols(w, gw):
    k, n = w.shape
    return w.reshape(k, n // gw, gw).transpose(1, 0, 2)


def _u_specs(tm, d, tp):
    hl = _U_HALO
    nh = tm // hl
    return [
        pl.BlockSpec((1, tm, d), lambda bi, i: (bi, i, 0)),
        pl.BlockSpec((1, hl, d), lambda bi, i: (bi, jnp.maximum(i * nh - 1, 0), 0)),
        pl.BlockSpec((1, hl, d), lambda bi, i: (bi, jnp.minimum((i + 1) * nh, tp // hl - 1), 0)),
    ]


def _ssm_conv(u3, wx_bf, w, bias, pad, inner):
    b, tp, d = u3.shape
    xw = wx_bf.shape[1]
    ktaps = w.shape[0]
    tm = _tile(tp, _FUSED_ROWS)
    hl = _U_HALO
    gn = (xw - inner) // 2
    assert inner % LANES == 0 and gn % LANES == 0 and (ktaps - 1) // 2 <= hl and tm % (8 * _ROW_STRIDE) == 0
    return pl.pallas_call(
        functools.partial(_ssm_conv_kernel, ktaps=ktaps, pad=pad, inner=inner),
        grid=(b, tp // tm),
        in_specs=_u_specs(tm, d, tp) + [
            pl.BlockSpec((xw // _PROJ_GROUP, d, _PROJ_GROUP), lambda bi, i: (0, 0, 0)),
            pl.BlockSpec((ktaps, xw), lambda bi, i: (0, 0)),
            pl.BlockSpec((1, xw), lambda bi, i: (0, 0)),
        ],
        out_specs=[
            pl.BlockSpec((1, tm, inner), lambda bi, i: (bi, i, 0)),
            pl.BlockSpec((1, inner, tm), lambda bi, i: (bi, 0, i)),
            pl.BlockSpec((1, tm, gn), lambda bi, i: (bi, i, 0)),
            pl.BlockSpec((1, tm, gn), lambda bi, i: (bi, i, 0)),
        ],
        out_shape=[
            jax.ShapeDtypeStruct((b, tp, inner), BF16),
            jax.ShapeDtypeStruct((b, inner, tp), BF16),
            jax.ShapeDtypeStruct((b, tp, gn), BF16),
            jax.ShapeDtypeStruct((b, tp, gn), BF16),
        ],
        scratch_shapes=[
            pltpu.VMEM((tm + 2 * hl, d), BF16),
            pltpu.VMEM((xw // LANES, tm + 2 * hl, LANES), F32),
            pltpu.VMEM((xw // LANES, tm, LANES), F32),
        ],
        compiler_params=_params("parallel", "arbitrary"),
        name="ssm_conv",
    )(u3, u3, u3, _group_cols(wx_bf, _PROJ_GROUP), w, bias.reshape(1, xw))


def _ssd_kernel(xst_ref, dtt_ref, bm_ref, cm_ref, a_ref, y_ref, s_ref, *, reverse, heads, groups):
    c = pl.program_id(1)
    p, n = SSM_HEAD_DIM, SSM_STATE
    hpg = heads // groups

    @pl.when(c == 0)
    def _():
        s_ref[...] = jnp.zeros_like(s_ref)

    dtt = dtt_ref[0]
    at = dtt * a_ref[...]
    li = lax.broadcasted_iota(jnp.int32, (CHUNK, CHUNK), 0)
    ri = lax.broadcasted_iota(jnp.int32, (CHUNK, CHUNK), 1)
    keep = (ri >= li) if reverse else (ri <= li)
    mx = keep.astype(F32)
    hi = lax.Precision.HIGHEST
    cum = lax.dot_general(mx, at, _NT, precision=hi, preferred_element_type=F32)
    cumt = lax.dot_general(at, mx, _NT, precision=hi, preferred_element_type=F32)
    edge = 0 if reverse else CHUNK - 1
    tott = cumt[:, edge:edge + 1]
    wt = dtt * jnp.exp(tott - cumt)
    etot = jnp.exp(tott)
    assert n == CHUNK

    for g in range(groups):
        cg = cm_ref[0, :, g * n:(g + 1) * n]
        bg = bm_ref[0, :, g * n:(g + 1) * n]
        cb = lax.dot_general(cg, bg, _NT, preferred_element_type=F32)
        cg32 = cg.astype(F32)
        xw, keep_s = [], []
        for j in range(hpg):
            h = g * hpg + j
            col = jnp.broadcast_to(cum[:, h:h + 1], (CHUNK, CHUNK))
            decay = jnp.exp(jnp.where(keep, col - cumt[h:h + 1, :], -jnp.inf))
            m = (cb * decay).astype(BF16)
            ce = (cg32 * jnp.exp(col)).astype(BF16)
            lhs = jnp.concatenate([m, ce], axis=1)
            xt = xst_ref[0, h * p:(h + 1) * p, :]
            sh = s_ref[g, j * p:(j + 1) * p, :]
            rhs_t = jnp.concatenate([(xt * dtt[h:h + 1, :]).astype(BF16), sh.astype(BF16)], axis=1)
            y_ref[0, :, h * p:(h + 1) * p] = lax.dot_general(
                lhs, rhs_t, _NT, preferred_element_type=F32).astype(y_ref.dtype)
            xw.append((xt * wt[h:h + 1, :]).astype(BF16))
            keep_s.append(sh * etot[h:h + 1, :])
        s_ref[g] = jnp.concatenate(keep_s, axis=0) + jnp.dot(
            jnp.concatenate(xw, axis=0), bg, preferred_element_type=F32)


def _ssd(xst, dtt_all, bm, cm, a_col_all, direction, heads):
    b, hp, tp = xst.shape
    nch = tp // CHUNK
    gn = bm.shape[2]
    reverse = direction == 1
    cidx = (lambda c: nch - 1 - c) if reverse else (lambda c: c)
    return pl.pallas_call(
        functools.partial(_ssd_kernel, reverse=reverse, heads=heads, groups=SSM_GROUPS),
        grid=(b, nch),
        in_specs=[
            pl.BlockSpec((1, hp, CHUNK), lambda i, c: (i, 0, cidx(c))),
            pl.BlockSpec((1, heads, CHUNK), lambda i, c: (i, direction, cidx(c))),
            pl.BlockSpec((1, CHUNK, gn), lambda i, c: (i, cidx(c), 0)),
            pl.BlockSpec((1, CHUNK, gn), lambda i, c: (i, cidx(c), 0)),
            pl.BlockSpec((heads, 1), lambda i, c: (direction, 0)),
        ],
        out_specs=pl.BlockSpec((1, CHUNK, hp), lambda i, c: (i, cidx(c), 0)),
        out_shape=jax.ShapeDtypeStruct((b, tp, hp), BF16),
        scratch_shapes=[pltpu.VMEM((SSM_GROUPS, heads // SSM_GROUPS * SSM_HEAD_DIM, SSM_STATE), F32)],
        compiler_params=_params("parallel", "arbitrary"),
        name="ssd_rev" if reverse else "ssd_fwd",
    )(xst, dtt_all, bm, cm, a_col_all)


def _merge_kernel(yf_ref, yb_ref, xs_ref, zs_ref, dsk_ref, wsn_ref, wso_ref, gc_ref, g1_ref, wo_ref, h0_ref,
                  wnf_ref, wrt_ref, h1_ref, hn_ref, aff_ref):
    y = yf_ref[...].astype(F32) + yb_ref[...].astype(F32) + dsk_ref[...] * xs_ref[...].astype(F32)
    y = y * zs_ref[...]
    ms = jnp.mean(y * y, axis=-1, keepdims=True)
    yn = (y * lax.rsqrt(ms + EPS) * wsn_ref[...]).astype(BF16)
    bs = jnp.dot(yn, wso_ref[...], preferred_element_type=F32)
    merged = gc_ref[...] + g1_ref[...] * bs
    h1 = h0_ref[...] + jnp.dot(merged.astype(BF16), wo_ref[...], preferred_element_type=F32)
    h1_ref[...] = h1
    ms1 = jnp.mean(h1 * h1, axis=-1, keepdims=True)
    hn = (h1 * lax.rsqrt(ms1 + EPS) * wnf_ref[...]).astype(BF16)
    hn_ref[...] = hn
    logits =lax.dot_general(wrt_ref[...], hn, _NT, preferred_element_type=F32)
    mx = jnp.max(logits, axis=0, keepdims=True)
    ex = jnp.exp(logits - mx)
    aff_ref[...] = ex / jnp.sum(ex, axis=0, keepdims=True)


def _merge(yf, yb, xs, zs, dskip, w_ssm_norm, w_ssm_out_bf, gc, gates, w_out_bf, h0, w_norm_ffn, w_router_t_bf):
    r, inner = yf.shape
    d = h0.shape[1]
    e = w_router_t_bf.shape[0]
    tm = _tile(r, 512)
    row = lambda i: (i, 0)
    fix = lambda i: (0, 0)
    return pl.pallas_call(
        _merge_kernel,
        grid=(r // tm,),
        in_specs=[
            pl.BlockSpec((tm, inner), row), pl.BlockSpec((tm, inner), row), pl.BlockSpec((tm, inner), row),
            pl.BlockSpec((tm, inner), row), pl.BlockSpec((1, inner), fix), pl.BlockSpec((1, inner), fix),
            pl.BlockSpec((inner, d), fix), pl.BlockSpec((tm, d), row), pl.BlockSpec((tm, d), row),
            pl.BlockSpec((d, d), fix), pl.BlockSpec((tm, d), row), pl.BlockSpec((1, d), fix),
            pl.BlockSpec((e, d), fix),
        ],
        out_specs=[pl.BlockSpec((tm, d), row), pl.BlockSpec((tm, d), row), pl.BlockSpec((e, tm), lambda i: (0, i))],
        out_shape=[jax.ShapeDtypeStruct((r, d), F32), jax.ShapeDtypeStruct((r, d), BF16),
                   jax.ShapeDtypeStruct((e, r), F32)],
        compiler_params=_params("parallel"),
        name="merge_router",
    )(yf, yb, xs, zs, dskip, w_ssm_norm, w_ssm_out_bf, gc, gates, w_out_bf, h0, w_norm_ffn, w_router_t_bf)


def _select_kernel(aff_ref, idx_ref, wts_ref, rank_ref, *, pad, cap):
    e, tp = aff_ref.shape
    n_hi, n_lo = idx_ref.shape[2], idx_ref.shape[3]
    aff = aff_ref[...]
    tpos = lax.broadcasted_iota(jnp.int32, (e, tp), 1)
    bits = jnp.where(tpos >= pad, pltpu.bitcast(aff, jnp.int32), -1)

    def refine(i, thr):
        cand = thr | lax.shift_left(jnp.int32(1), 30 - i)
        cnt = jnp.sum((bits >= cand).astype(jnp.int32), axis=1, keepdims=True)
        return jnp.where(cnt >= cap, cand, thr)

    thr = lax.fori_loop(0, 31, refine, jnp.zeros((e, 1), jnp.int32))
    gt = bits > thr
    eq = bits == thr
    need = cap - jnp.sum(gt.astype(jnp.int32), axis=1, keepdims=True)

    ri = lax.broadcasted_iota(jnp.int32, (CHUNK, CHUNK), 0)
    ci = lax.broadcasted_iota(jnp.int32, (CHUNK, CHUNK), 1)
    upper = (ri <= ci).astype(BF16)

    def prefix(flags):
        off = jnp.zeros((e, 1), F32)
        for k in range(tp // CHUNK):
            blk = flags[:, k * CHUNK:(k + 1) * CHUNK].astype(BF16)
            inc = jnp.dot(blk, upper, preferred_element_type=F32) + off
            rank_ref[:, k * CHUNK:(k + 1) * CHUNK] = inc
            off = inc[:, CHUNK - 1:CHUNK]

    eqf = jnp.where(eq, 1.0, 0.0)
    prefix(eqf)
    sel = gt | (eq & ((rank_ref[...] - eqf) < need.astype(F32)))
    prefix(jnp.where(sel, 1.0, 0.0))
    rank_ref[...] = jnp.where(sel, rank_ref[...], 0.0) - 1.0

    t1 = lax.broadcasted_iota(jnp.int32, (1, tp), 1)
    t_hi = lax.shift_right_logical(t1, 6).astype(F32)
    t_lo = (t1 & 63).astype(F32)
    hh = lax.broadcasted_iota(jnp.int32, (n_hi, 1), 0).astype(F32)
    ll = lax.broadcasted_iota(jnp.int32, (n_lo, 1), 0).astype(F32)
    jj = lax.broadcasted_iota(jnp.int32, (n_hi, n_lo), 0) * n_lo + lax.broadcasted_iota(jnp.int32, (n_hi, n_lo), 1)

    def emit(ei, carry):
        s0 = rank_ref[pl.ds(ei, 1), :]
        a0 = aff_ref[pl.ds(ei, 1), :]
        hi = jnp.floor(s0 * (1.0 / n_lo))
        lo = s0 - n_lo * hi
        a1 = a0.astype(BF16).astype(F32)
        a2 = (a0 - a1).astype(BF16).astype(F32)
        a3 = (a0 - a1 - a2).astype(BF16).astype(F32)
        in_hi = hi == hh
        parts = [jnp.where(in_hi, v, 0.0) for v in (t_hi, t_lo, a1, a2, a3)]
        parts.append(jnp.zeros((n_hi, tp), F32))
        lhs = jnp.concatenate(parts, axis=0).astype(BF16)
        rhs = jnp.where(lo == ll, 1.0, 0.0).astype(BF16)
        res = lax.dot_general(lhs, rhs, _NT, preferred_element_type=F32)
        tok = (64.0 * res[0:n_hi] + res[n_hi:2 * n_hi]).astype(jnp.int32)
        idx_ref[0, ei] = jnp.where(jj >= cap, jj - cap, tok)
        wts_ref[0, ei] = res[2 * n_hi:3 * n_hi] + res[3 * n_hi:4 * n_hi] + res[4 * n_hi:5 * n_hi]
        return carry

    lax.fori_loop(0, e, emit, 0)


_SEL_HI, _SEL_LO = 8, 128


def _select(aff_t, b, tp, pad, cap):
    e = aff_t.shape[0]
    assert cap <= _SEL_HI * _SEL_LO and tp <= 64 * 256
    return pl.pallas_call(
        functools.partial(_select_kernel, pad=pad, cap=cap),
        grid=(b,),
        in_specs=[pl.BlockSpec((e, tp), lambda i: (0, i))],
        out_specs=[
            pl.BlockSpec((1, e, _SEL_HI, _SEL_LO), lambda i: (i, 0, 0, 0)),
            pl.BlockSpec((1, e, _SEL_HI, _SEL_LO), lambda i: (i, 0, 0, 0)),
        ],
        out_shape=[jax.ShapeDtypeStruct((b, e, _SEL_HI, _SEL_LO), jnp.int32),
                   jax.ShapeDtypeStruct((b, e, _SEL_HI, _SEL_LO), F32)],
        scratch_shapes=[pltpu.VMEM((e, tp), F32)],
        compiler_params=_params("parallel"),
        name="select",
    )(aff_t)


def _gather_kernel(idx_ref, hn_ref, xn_ref, *, n_exp):
    bi, ei = pl.program_id(0), pl.program_id(1)
    capp = xn_ref.shape[2]
    base = (bi * n_exp + ei) * capp

    def copy(j, carry):
        t = idx_ref[base + j]
        xn_ref[0, 0, pl.ds(j, 1), :] = hn_ref[0, pl.ds(t, 1), :]
        return carry

    lax.fori_loop(0, capp, copy, 0, unroll=8)


def _gather(idx_flat, hn_words, n_exp, capp):
    b, tp, dw = hn_words.shape
    return pl.pallas_call(
        functools.partial(_gather_kernel, n_exp=n_exp),
        grid_spec=pltpu.PrefetchScalarGridSpec(
            num_scalar_prefetch=1,
            grid=(b, n_exp),
            in_specs=[pl.BlockSpec((1, tp, dw), lambda i, e, idx: (i, 0, 0))],
            out_specs=pl.BlockSpec((1, 1, capp, dw), lambda i, e, idx: (i, e, 0, 0)),
        ),
        out_shape=jax.ShapeDtypeStruct((b, n_exp, capp, dw), jnp.uint32),
        compiler_params=_params("parallel", "arbitrary"),
        name="moe_gather",
    )(idx_flat, hn_words)


def _ffn_kernel(xn_ref, wts_ref, wg_ref, wu_ref, wd_ref, y_ref):
    f = pl.program_id(2)
    nb, capp, d = xn_ref.shape[0], xn_ref.shape[2], xn_ref.shape[3]
    xn = xn_ref[...].reshape(nb * capp, d)
    hg = jnp.dot(xn, wg_ref[0].astype(BF16), preferred_element_type=F32)
    hu = jnp.dot(xn, wu_ref[0].astype(BF16), preferred_element_type=F32)
    act = (_silu(hg) * hu).astype(BF16)
    yp = jnp.dot(act, wd_ref[0].astype(BF16), preferred_element_type=F32).reshape(nb, 1, capp, d)

    @pl.when(f == 0)
    def _():
        y_ref[...] = yp

    @pl.when(f > 0)
    def _():
        y_ref[...] += yp

    @pl.when(f == pl.num_programs(2) - 1)
    def _():
        y_ref[...] = y_ref[...] * wts_ref[...]


def _ffn(xn, wts, w_gate, w_up, w_down):
    b, n_exp, capp, d = xn.shape
    ff = w_gate.shape[2]
    nb = 2 if b % 2 == 0 else 1
    tf = _tile(ff, 512)
    return pl.pallas_call(
        _ffn_kernel,
        grid=(n_exp, b // nb, ff // tf),
        in_specs=[
            pl.BlockSpec((nb, 1, capp, d), lambda e, m, f: (m, e, 0, 0)),
            pl.BlockSpec((nb, 1, capp, 1), lambda e, m, f: (m, e, 0, 0)),
            pl.BlockSpec((1, d, tf), lambda e, m, f: (e, 0, f)),
            pl.BlockSpec((1, d, tf), lambda e, m, f: (e, 0, f)),
            pl.BlockSpec((1, tf, d), lambda e, m, f: (e, f, 0)),
        ],
        out_specs=pl.BlockSpec((nb, 1, capp, d), lambda e, m, f: (m, e, 0, 0)),
        out_shape=jax.ShapeDtypeStruct((b, n_exp, capp, d), F32),
        compiler_params=_params("parallel", "parallel", "arbitrary"),
        name="moe_ffn",
    )(xn, wts, w_gate, w_up, w_down)


_SCATTER_GROUP = 8


def _scatter_kernel(idx_ref, y_ref, o_ref, *, n_exp):
    bi, ei = pl.program_id(0), pl.program_id(1)
    capp = y_ref.shape[2]
    base = (bi * n_exp + ei) * capp

    @pl.when(ei == 0)
    def _():
        o_ref[...] = jnp.zeros_like(o_ref)

    def add_group(g, carry):
        j0 = pl.multiple_of(g * _SCATTER_GROUP, _SCATTER_GROUP)
        toks = [idx_ref[base + j0 + k] for k in range(_SCATTER_GROUP)]
        rows = [o_ref[0, pl.ds(t, 1), :] for t in toks]
        ytile = y_ref[0, 0, pl.ds(j0, _SCATTER_GROUP), :]
        for k, t in enumerate(toks):
            o_ref[0, pl.ds(t, 1), :] = rows[k] + ytile[k:k + 1, :]
        return carry

    lax.fori_loop(0, capp // _SCATTER_GROUP, add_group, 0)


def _scatter(idx_flat, y, tp):
    b, n_exp, capp, d = y.shape
    return pl.pallas_call(
        functools.partial(_scatter_kernel, n_exp=n_exp),
        grid_spec=pltpu.PrefetchScalarGridSpec(
            num_scalar_prefetch=1,
            grid=(b, n_exp),
            in_specs=[pl.BlockSpec((1, 1, capp, d), lambda i, e, idx: (i, e, 0, 0))],
            out_specs=pl.BlockSpec((1, tp, d), lambda i, e, idx: (i, 0, 0)),
        ),
        out_shape=jax.ShapeDtypeStruct((b, tp, d), F32),
        compiler_params=_params("parallel", "arbitrary"),
        name="moe_scatter",
    )(idx_flat, y)


def _final_kernel(h1_ref, dl_ref, w_ref, o_ref):
    h = h1_ref[...] + dl_ref[...]
    ms = jnp.mean(h * h, axis=-1, keepdims=True)
    o_ref[...] = h * lax.rsqrt(ms + EPS) * w_ref[...]


def _final(h1, delta, w_norm, b):
    r, d = h1.shape
    tp = r // b
    seq = tp - CHUNK
    tm = _tile(seq, 512)
    n_t = seq // tm
    src = pl.BlockSpec((pl.Element(tm), pl.Element(d)),
                       lambda i, c: (pl.multiple_of(i * tp + c * tm + CHUNK, CHUNK), 0))
    return pl.pallas_call(
        _final_kernel,
        grid=(b, n_t),
        in_specs=[src, src, pl.BlockSpec((1, d), lambda i, c: (0, 0))],
        out_specs=pl.BlockSpec((tm, d), lambda i, c: (i * n_t + c, 0)),
        out_shape=jax.ShapeDtypeStruct((b * seq, d), F32),
        compiler_params=_params("parallel", "parallel"),
        name="final_norm",
    )(h1, delta, w_norm).reshape(b, seq, d)


def kernel(x, meta_tokens, w_norm_mix, w_in, w_conf_dw, b_conf_dw, conf_ln_g, conf_ln_b, w_conf_out,
           w_ssm_conv, b_ssm_conv, ssm_dt_bias, ssm_a_log, ssm_d, w_ssm_norm, w_ssm_out, w_out,
           w_norm_ffn, w_router, w_exp_gate, w_exp_up, w_exp_down, w_norm_final):
    b, seq, d = x.shape
    depth = w_in.shape[0]
    assert depth == 1 and seq % CHUNK == 0
    n_meta = meta_tokens.shape[0]
    pad = CHUNK - n_meta
    lt = n_meta + seq
    tp = seq + CHUNK
    r = b * tp
    cw = w_conf_dw.shape[2]
    heads = ssm_d.shape[1]
    inner = heads * SSM_HEAD_DIM
    xw = w_ssm_conv.shape[2]
    n_exp = w_router.shape[2]
    cap = CAPACITY_FACTOR * lt // n_exp
    capp = -(-cap // 8) * 8
    off_z = 2 * cw
    off_xbc = off_z + inner
    off_dt = off_xbc + xw
    off_gate = off_dt + 2 * heads

    w_in0 = w_in[0]
    wa = w_in0[:, 0:cw].astype(BF16)
    wg = w_in0[:, cw:off_z].astype(BF16)
    wz = w_in0[:, off_z:off_xbc].astype(BF16)
    wxbc = w_in0[:, off_xbc:off_dt].astype(BF16)
    wdt_t = w_in0[:, off_dt:off_gate].T.astype(BF16)
    wgate = w_in0[:, off_gate:].astype(BF16)
    dt_bias_col = ssm_dt_bias[0].reshape(2 * heads, 1).astype(F32)
    a_col = (-jnp.exp(ssm_a_log[0].astype(F32))).reshape(2 * heads, 1)
    dskip = jnp.repeat(ssm_d[0].astype(F32), SSM_HEAD_DIM).reshape(1, inner)

    u2, h0 = _prep(x, meta_tokens, w_norm_mix[0])
    u3 = u2.reshape(b, tp, d)
    zs = _mm(u2, wz, "silu", out_dtype=BF16)
    gate_ssm = _mm(u2, wgate[:, d:], "sigmoid", out_dtype=BF16)
    dtt = _dt_proj(u3, wdt_t, dt_bias_col, pad)

    gc = _conf_branch(u3, wa, wg, w_conf_dw[0], b_conf_dw[0], conf_ln_g[0], conf_ln_b[0],
                      w_conf_out[0].astype(BF16), wgate[:, :d])

    xs, xst, bm, cm = _ssm_conv(u3, wxbc, w_ssm_conv[0], b_ssm_conv[0], pad, inner)
    yf = _ssd(xst, dtt, bm, cm, a_col, 0, heads)
    yb = _ssd(xst, dtt, bm, cm, a_col, 1, heads)

    h1, hn, aff_t = _merge(yf.reshape(r, inner), yb.reshape(r, inner), xs.reshape(r, inner), zs, dskip,
                           w_ssm_norm[0].reshape(1, inner).astype(F32), w_ssm_out[0].astype(BF16),
                           gc.reshape(r, d), gate_ssm, w_out[0].astype(BF16), h0,
                           w_norm_ffn[0].reshape(1, d).astype(F32), w_router[0].T.astype(BF16))

    idx, wts = _select(aff_t, b, tp, pad, cap)
    idx_flat = idx.reshape(b, n_exp, -1)[:, :, :capp].reshape(b * n_exp * capp)
    wts = wts.reshape(b, n_exp, -1)[:, :, :capp].reshape(b, n_exp, capp, 1)
    hn_words = lax.bitcast_convert_type(hn.reshape(b, tp, d // 2, 2), jnp.uint32)
    xn_words = _gather(idx_flat, hn_words, n_exp, capp)
    xn = lax.bitcast_convert_type(xn_words, BF16).reshape(b, n_exp, capp, d)
    y = _ffn(xn, wts, w_exp_gate[0], w_exp_up[0], w_exp_down[0])
    delta = _scatter(idx_flat, y, tp)
    return _final(h1, delta.reshape(r, d), w_norm_final.reshape(1, d).astype(F32), b)
```

```python
import functools

import jax
import jax.numpy as jnp
from jax import lax
from jax.experimental import pallas as pl
from jax.experimental.pallas import tpu as pltpu

CHUNK = 128
SSM_STATE = 128
SSM_HEAD_DIM = 64
SSM_GROUPS = 4
CAPACITY_FACTOR = 2
EPS = 1e-6
V7X_VMEM_LIMIT = 52 * 1024 * 1024
F32 = jnp.float32
BF16 = jnp.bfloat16

_NT = (((1,), (1,)), ((), ()))


def _params(*sem):
    return pltpu.CompilerParams(dimension_semantics=sem, vmem_limit_bytes=V7X_VMEM_LIMIT)


def _tile(n, cap, mult=128):
    best = None
    for t in range(mult, min(n, cap) + 1, mult):
        if n % t == 0:
            best = t
    assert best is not None, (n, cap, mult)
    return best


def _sigmoid(v):
    return 0.5 * jnp.tanh(0.5 * v) + 0.5


def _silu(v):
    h = 0.5 * v
    return h + h * jnp.tanh(h)


def _prep_kernel(x_ref, meta_ref, w_ref, u_ref, h_ref, *, n_meta):
    c = pl.program_id(1)
    w = w_ref[...]

    def emit(rows):
        h_ref[...] = rows
        ms = jnp.mean(rows * rows, axis=-1, keepdims=True)
        u_ref[...] = (rows * lax.rsqrt(ms + EPS) * w).astype(u_ref.dtype)

    @pl.when(c == 0)
    def _():
        tm, d = x_ref.shape
        emit(jnp.concatenate([jnp.zeros((CHUNK - n_meta, d), F32), meta_ref[...], x_ref[0:tm - CHUNK, :]], axis=0))

    @pl.when(c > 0)
    def _():
        emit(x_ref[...])


def _prep(x, meta, w_norm):
    b, seq, d = x.shape
    n_meta = meta.shape[0]
    tp = seq + CHUNK
    tm = _tile(tp, _FUSED_ROWS)
    n_t = tp // tm
    assert tm > CHUNK
    dst = pl.BlockSpec((tm, d), lambda i, c: (i * n_t + c, 0))
    return pl.pallas_call(
        functools.partial(_prep_kernel, n_meta=n_meta),
        grid=(b, n_t),
        in_specs=[
            pl.BlockSpec((pl.Element(tm), pl.Element(d)),
                         lambda i, c: (pl.multiple_of(i * seq + jnp.maximum(c * tm - CHUNK, 0), CHUNK), 0)),
            pl.BlockSpec((n_meta, d), lambda i, c: (0, 0)),
            pl.BlockSpec((1, d), lambda i, c: (0, 0)),
        ],
        out_specs=[dst, dst],
        out_shape=[jax.ShapeDtypeStruct((b * tp, d), BF16), jax.ShapeDtypeStruct((b * tp, d), F32)],
        compiler_params=_params("parallel", "arbitrary"),
        name="prep",
    )(x.reshape(b * seq, d), meta, w_norm.reshape(1, d))


def _mm_kernel(u_ref, w_ref, o_ref, *, act):
    acc = jnp.dot(u_ref[...], w_ref[...], preferred_element_type=F32)
    if act == "silu":
        acc = _silu(acc)
    elif act == "sigmoid":
        acc = _sigmoid(acc)
    o_ref[...] = acc.astype(o_ref.dtype)


def _mm(u2, w, act, out_dtype=F32, tn=1024):
    r, k = u2.shape
    n = w.shape[1]
    tm = _tile(r, 1024)
    tn = _tile(n, tn)
    return pl.pallas_call(
        functools.partial(_mm_kernel, act=act),
        grid=(r // tm, n // tn),
        in_specs=[pl.BlockSpec((tm, k), lambda i, j: (i, 0)), pl.BlockSpec((k, tn), lambda i, j: (0, j))],
        out_specs=pl.BlockSpec((tm, tn), lambda i, j: (i, j)),
        out_shape=jax.ShapeDtypeStruct((r, n), out_dtype),
        compiler_params=_params("parallel", "arbitrary"),
        name="proj_" + str(act),
    )(u2, w)


def _dt_kernel(u_ref, wt_ref, bias_ref, o_ref, *, pad):
    raw = lax.dot_general(wt_ref[...], u_ref[0], _NT, preferred_element_type=F32)
    v = raw + bias_ref[...]
    sp = jnp.maximum(v, 0.0) + jnp.log1p(jnp.exp(-jnp.abs(v)))
    t = lax.broadcasted_iota(jnp.int32, sp.shape, 1)
    o_ref[0] = jnp.where(t >= pad, sp, 0.0)


def _dt_proj(u3, w_dt_t, bias_col, pad):
    b, tp, d = u3.shape
    h2 = w_dt_t.shape[0]
    return pl.pallas_call(
        functools.partial(_dt_kernel, pad=pad),
        grid=(b,),
        in_specs=[
            pl.BlockSpec((1, tp, d), lambda i: (i, 0, 0)),
            pl.BlockSpec((h2, d), lambda i: (0, 0)),
            pl.BlockSpec((h2, 1), lambda i: (0, 0)),
        ],
        out_specs=pl.BlockSpec((1, h2, tp), lambda i: (i, 0, 0)),
        out_shape=jax.ShapeDtypeStruct((b, h2, tp), F32),
        compiler_params=_params("parallel"),
        name="proj_dt",
    )(u3, w_dt_t, bias_col)


LANES = 128
_ROW_STRIDE = 4
_U_HALO = 16
_FUSED_ROWS = 384
_PROJ_GROUP = 256


def _fill_lhs(lhs_ref, cur_ref, prev_ref, next_ref):
    i = pl.program_id(1)
    tm, hl = cur_ref.shape[1], prev_ref.shape[1]
    zero = jnp.zeros(prev_ref.shape[1:], prev_ref.dtype)
    lhs_ref[0:hl, :] = jnp.where(i > 0, prev_ref[0], zero)
    lhs_ref[hl:hl + tm, :] = cur_ref[0]
    lhs_ref[hl + tm:hl + tm + hl, :] = jnp.where(i < pl.num_programs(1) - 1, next_ref[0], zero)


def _conf_kernel(cur_ref, prev_ref, next_ref, wa_ref, wg_ref, wdw_ref, bdw_ref, g_ref, b_ref, wout_ref, wgate_ref,
                 o_ref, lhs_ref, win_ref, conv_ref, *, ktaps):
    tm = cur_ref.shape[1]
    hl = _U_HALO
    half = (ktaps - 1) // 2
    spg = _PROJ_GROUP // LANES
    ns = wa_ref.shape[1] // LANES
    _fill_lhs(lhs_ref, cur_ref, prev_ref, next_ref)
    blk = 8 * _ROW_STRIDE
    for s in range(ns):
        if s % spg == 0:
            lhs = lhs_ref[...]
            gs = slice(s * LANES, s * LANES + _PROJ_GROUP)
            a = jnp.dot(lhs, wa_ref[:, gs], preferred_element_type=F32)
            g = jnp.dot(lhs, wg_ref[:, gs], preferred_element_type=F32)
            glu = a * _sigmoid(g)
            for q in range(spg):
                win_ref[s + q] = glu[:, q * LANES:(q + 1) * LANES]
        ls = slice(s * LANES, (s + 1) * LANES)
        bias = jnp.broadcast_to(bdw_ref[:, ls], (8, LANES))
        for rb in range(tm // blk):
            for j in range(_ROW_STRIDE):
                r0 = rb * blk + j
                acc = bias
                for k in range(ktaps):
                    rows = win_ref[s, pl.ds(r0 + hl - half + k, 8, stride=_ROW_STRIDE), :]
                    acc = acc + rows * wdw_ref[k:k + 1, ls]
                conv_ref[s, pl.ds(r0, 8, stride=_ROW_STRIDE), :] = acc
    cv = jnp.concatenate([conv_ref[s] for s in range(ns)], axis=1)
    mu = jnp.mean(cv, axis=-1, keepdims=True)
    xc = cv - mu
    var = jnp.mean(xc * xc, axis=-1, keepdims=True)
    y = xc * lax.rsqrt(var + EPS) * g_ref[...] + b_ref[...]
    y = _silu(y)
    br =jnp.dot(y.astype(BF16), wout_ref[...], preferred_element_type=F32)
    gate = _sigmoid(jnp.dot(cur_ref[0], wgate_ref[...], preferred_element_type=F32))
    o_ref[0] = gate * br


def _conf_branch(u3, wa_bf, wg_bf, w_dw, b_dw, ln_g, ln_b, w_out_bf, w_gate_bf):
    b, tp, d = u3.shape
    cw = wa_bf.shape[1]
    ktaps = w_dw.shape[0]
    hl = _U_HALO
    tm = _tile(tp, _FUSED_ROWS)
    assert (ktaps - 1) // 2 <= hl and cw % _PROJ_GROUP == 0 and tm % (8 * _ROW_STRIDE) == 0
    fix2 = lambda bi, i: (0, 0)
    return pl.pallas_call(
        functools.partial(_conf_kernel, ktaps=ktaps),
        grid=(b, tp // tm),
        in_specs=_u_specs(tm, d, tp) + [
            pl.BlockSpec((d, cw), fix2),
            pl.BlockSpec((d, cw), fix2),
            pl.BlockSpec((ktaps, cw), fix2),
            pl.BlockSpec((1, cw), fix2),
            pl.BlockSpec((1, cw), fix2),
            pl.BlockSpec((1, cw), fix2),
            pl.BlockSpec((cw, d), fix2),
            pl.BlockSpec((d, d), fix2),
        ],
        out_specs=pl.BlockSpec((1, tm, d), lambda bi, i: (bi, i, 0)),
        out_shape=jax.ShapeDtypeStruct((b, tp, d), F32),
        scratch_shapes=[
            pltpu.VMEM((tm + 2 * hl, d), BF16),
            pltpu.VMEM((cw // LANES, tm + 2 * hl, LANES), F32),
            pltpu.VMEM((cw // LANES, tm, LANES), F32),
        ],
        compiler_params=_params("parallel", "arbitrary"),
        name="conf_branch",
    )(u3, u3, u3, wa_bf, wg_bf, w_dw, b_dw.reshape(1, cw),
      ln_g.reshape(1, cw), ln_b.reshape(1, cw), w_out_bf, w_gate_bf)


def _ssm_conv_kernel(cur_ref, prev_ref, next_ref, wx_ref, w_ref, b_ref, xs_ref, xst_ref, bm_ref, cm_ref,
                     lhs_ref, win_ref, conv_ref, *, ktaps, pad, inner):
    c = pl.program_id(1)
    tm = cur_ref.shape[1]
    hl = _U_HALO
    half = (ktaps - 1) // 2
    spg = _PROJ_GROUP // LANES
    xw = wx_ref.shape[1]
    ns = xw // LANES
    _fill_lhs(lhs_ref, cur_ref, prev_ref, next_ref)
    blk = 8 * _ROW_STRIDE
    row_in_vreg = _ROW_STRIDE * lax.broadcasted_iota(jnp.int32, (8, 1), 0)
    for s in range(ns):
        if s % spg == 0:
            res = jnp.dot(lhs_ref[...], wx_ref[:, s * LANES:s * LANES + _PROJ_GROUP], preferred_element_type=F32)
            for q in range(spg):
                win_ref[s + q] = res[:, q * LANES:(q + 1) * LANES]
        ls = slice(s * LANES, (s + 1) * LANES)
        taps = [jnp.broadcast_to(w_ref[k:k + 1, ls], (8, LANES)) for k in range(ktaps)]
        bias = jnp.broadcast_to(b_ref[:, ls], (8, LANES))
        for rb in range(tm // blk):
            for j in range(_ROW_STRIDE):
                r0 = rb * blk + j
                acc = bias
                for k in range(ktaps):
                    acc = acc + win_ref[s, pl.ds(r0 + hl - half + k, 8, stride=_ROW_STRIDE), :] * taps[k]
                acc = _silu(acc)
                if rb * blk < pad:
                    acc = jnp.where((c * tm + r0 + row_in_vreg) >= pad, acc, 0.0)
                conv_ref[s, pl.ds(r0, 8, stride=_ROW_STRIDE), :] = acc
    gn = (xw - inner) // 2
    for s in range(ns):
        ls = slice(s * LANES, (s + 1) * LANES)
        slab = conv_ref[s]
        if s * LANES < inner:
            xs_ref[0, :, ls] = slab.astype(BF16)
            xst_ref[0, ls, :] = slab.T.astype(BF16)
        elif s * LANES < inner + gn:
            bm_ref[0, :, s * LANES - inner:(s + 1) * LANES - inner] = slab.astype(BF16)
        else:
            cm_ref[0, :, s * LANES - inner - gn:(s + 1) * LANES - inner - gn] = slab.astype(BF16)


def _u_specs(tm, d, tp):
    hl = _U_HALO
    nh = tm // hl
    return [
        pl.BlockSpec((1, tm, d), lambda bi, i: (bi, i, 0)),
        pl.BlockSpec((1, hl, d), lambda bi, i: (bi, jnp.maximum(i * nh - 1, 0), 0)),
        pl.BlockSpec((1, hl, d), lambda bi, i: (bi, jnp.minimum((i + 1) * nh, tp // hl - 1), 0)),
    ]


def _ssm_conv(u3, wx_bf, w, bias, pad, inner):
    b, tp, d = u3.shape
    xw = wx_bf.shape[1]
    ktaps = w.shape[0]
    tm = _tile(tp, _FUSED_ROWS)
    hl = _U_HALO
    gn = (xw - inner) // 2
    assert inner % LANES == 0 and gn % LANES == 0 and (ktaps - 1) // 2 <= hl and tm % (8 * _ROW_STRIDE) == 0
    return pl.pallas_call(
        functools.partial(_ssm_conv_kernel, ktaps=ktaps, pad=pad, inner=inner),
        grid=(b, tp // tm),
        in_specs=_u_specs(tm, d, tp) + [
            pl.BlockSpec((d, xw), lambda bi, i: (0, 0)),
            pl.BlockSpec((ktaps, xw), lambda bi, i: (0, 0)),
            pl.BlockSpec((1, xw), lambda bi, i: (0, 0)),
        ],
        out_specs=[
            pl.BlockSpec((1, tm, inner), lambda bi, i: (bi, i, 0)),
            pl.BlockSpec((1, inner, tm), lambda bi, i: (bi, 0, i)),
            pl.BlockSpec((1, tm, gn), lambda bi, i: (bi, i, 0)),
            pl.BlockSpec((1, tm, gn), lambda bi, i: (bi, i, 0)),
        ],
        out_shape=[
            jax.ShapeDtypeStruct((b, tp, inner), BF16),
            jax.ShapeDtypeStruct((b, inner, tp), BF16),
            jax.ShapeDtypeStruct((b, tp, gn), BF16),
            jax.ShapeDtypeStruct((b, tp, gn), BF16),
        ],
        scratch_shapes=[
            pltpu.VMEM((tm + 2 * hl, d), BF16),
            pltpu.VMEM((xw // LANES, tm + 2 * hl, LANES), F32),
            pltpu.VMEM((xw // LANES, tm, LANES), F32),
        ],
        compiler_params=_params("parallel", "arbitrary"),
        name="ssm_conv",
    )(u3, u3, u3, wx_bf, w, bias.reshape(1, xw))


def _ssd_kernel(xst_ref, dtt_ref, bm_ref, cm_ref, a_ref, y_ref, s_ref, *, reverse, heads, groups):
    c = pl.program_id(1)
    p, n = SSM_HEAD_DIM, SSM_STATE
    hpg = heads // groups

    @pl.when(c == 0)
    def _():
        s_ref[...] = jnp.zeros_like(s_ref)

    dtt = dtt_ref[0]
    at = dtt * a_ref[...]
    li = lax.broadcasted_iota(jnp.int32, (CHUNK, CHUNK), 0)
    ri = lax.broadcasted_iota(jnp.int32, (CHUNK, CHUNK), 1)
    keep = (ri >= li) if reverse else (ri <= li)
    mx = keep.astype(F32)
    hi = lax.Precision.HIGHEST
    cum = lax.dot_general(mx, at, _NT, precision=hi, preferred_element_type=F32)
    cumt = lax.dot_general(at, mx, _NT, precision=hi, preferred_element_type=F32)
    edge = 0 if reverse else CHUNK - 1
    tott = cumt[:, edge:edge + 1]
    wt = dtt * jnp.exp(tott - cumt)
    etot = jnp.exp(tott)
    assert n == CHUNK

    for g in range(groups):
        cg = cm_ref[0, :, g * n:(g + 1) * n]
        bg = bm_ref[0, :, g * n:(g + 1) * n]
        cb = lax.dot_general(cg, bg, _NT, preferred_element_type=F32)
        cg32 = cg.astype(F32)
        xw, keep_s = [], []
        for j in range(hpg):
            h = g * hpg + j
            col = jnp.broadcast_to(cum[:, h:h + 1], (CHUNK, CHUNK))
            decay = jnp.exp(jnp.where(keep, col - cumt[h:h + 1, :], -jnp.inf))
            m = (cb * decay).astype(BF16)
            ce = (cg32 * jnp.exp(col)).astype(BF16)
            lhs = jnp.concatenate([m, ce], axis=1)
            xt = xst_ref[0, h * p:(h + 1) * p, :]
            sh = s_ref[g, j * p:(j + 1) * p, :]
            rhs_t = jnp.concatenate([(xt * dtt[h:h + 1, :]).astype(BF16), sh.astype(BF16)], axis=1)
            y_ref[0, :, h * p:(h + 1) * p] = lax.dot_general(
                lhs, rhs_t, _NT, preferred_element_type=F32).astype(y_ref.dtype)
            xw.append((xt * wt[h:h + 1, :]).astype(BF16))
            keep_s.append(sh * etot[h:h + 1, :])
        s_ref[g] = jnp.concatenate(keep_s, axis=0) + jnp.dot(
            jnp.concatenate(xw, axis=0), bg, preferred_element_type=F32)


def _ssd(xst, dtt_all, bm, cm, a_col_all, direction, heads):
    b, hp, tp = xst.shape
    nch = tp // CHUNK
    gn = bm.shape[2]
    reverse = direction == 1
    cidx = (lambda c: nch - 1 - c) if reverse else (lambda c: c)
    return pl.pallas_call(
        functools.partial(_ssd_kernel, reverse=reverse, heads=heads, groups=SSM_GROUPS),
        grid=(b, nch),
        in_specs=[
            pl.BlockSpec((1, hp, CHUNK), lambda i, c: (i, 0, cidx(c))),
            pl.BlockSpec((1, heads, CHUNK), lambda i, c: (i, direction, cidx(c))),
            pl.BlockSpec((1, CHUNK, gn), lambda i, c: (i, cidx(c), 0)),
            pl.BlockSpec((1, CHUNK, gn), lambda i, c: (i, cidx(c), 0)),
            pl.BlockSpec((heads, 1), lambda i, c: (direction, 0)),
        ],
        out_specs=pl.BlockSpec((1, CHUNK, hp), lambda i, c: (i, cidx(c), 0)),
        out_shape=jax.ShapeDtypeStruct((b, tp, hp), BF16),
        scratch_shapes=[pltpu.VMEM((SSM_GROUPS, heads // SSM_GROUPS * SSM_HEAD_DIM, SSM_STATE), F32)],
        compiler_params=_params("parallel", "arbitrary"),
        name="ssd_rev" if reverse else "ssd_fwd",
    )(xst, dtt_all, bm, cm, a_col_all)


def _merge_kernel(yf_ref, yb_ref, xs_ref, zs_ref, dsk_ref, wsn_ref, wso_ref, gc_ref, g1_ref, wo_ref, h0_ref,
                  wnf_ref, wrt_ref, h1_ref, hn_ref, aff_ref):
    y = yf_ref[...].astype(F32) + yb_ref[...].astype(F32) + dsk_ref[...] * xs_ref[...].astype(F32)
    y = y * zs_ref[...]
    ms = jnp.mean(y * y, axis=-1, keepdims=True)
    yn = (y * lax.rsqrt(ms + EPS) * wsn_ref[...]).astype(BF16)
    bs = jnp.dot(yn, wso_ref[...], preferred_element_type=F32)
    merged = gc_ref[...] + g1_ref[...] * bs
    h1 = h0_ref[...] + jnp.dot(merged.astype(BF16), wo_ref[...], preferred_element_type=F32)
    h1_ref[...] = h1
    ms1 = jnp.mean(h1 * h1, axis=-1, keepdims=True)
    hn32 = h1 * lax.rsqrt(ms1 + EPS) * wnf_ref[...]
    hn = hn32.astype(BF16)
    half = hn32.shape[1] // 2
    hn_ref[...] = _pack_bf16_pair(hn32[:, :half], hn32[:, half:])
    logits =lax.dot_general(wrt_ref[...], hn, _NT, preferred_element_type=F32)
    mx = jnp.max(logits, axis=0, keepdims=True)
    ex = jnp.exp(logits - mx)
    aff_ref[...] = ex / jnp.sum(ex, axis=0, keepdims=True)


def _merge(yf, yb, xs, zs, dskip, w_ssm_norm, w_ssm_out_bf, gc, gates, w_out_bf, h0, w_norm_ffn, w_router_t_bf):
    r, inner = yf.shape
    d = h0.shape[1]
    e = w_router_t_bf.shape[0]
    tm = _tile(r, 512)
    row = lambda i: (i, 0)
    fix = lambda i: (0, 0)
    return pl.pallas_call(
        _merge_kernel,
        grid=(r // tm,),
        in_specs=[
            pl.BlockSpec((tm, inner), row), pl.BlockSpec((tm, inner), row), pl.BlockSpec((tm, inner), row),
            pl.BlockSpec((tm, inner), row), pl.BlockSpec((1, inner), fix), pl.BlockSpec((1, inner), fix),
            pl.BlockSpec((inner, d), fix), pl.BlockSpec((tm, d), row), pl.BlockSpec((tm, d), row),
            pl.BlockSpec((d, d), fix), pl.BlockSpec((tm, d), row), pl.BlockSpec((1, d), fix),
            pl.BlockSpec((e, d), fix),
        ],
        out_specs=[pl.BlockSpec((tm, d), row), pl.BlockSpec((tm, d // 2), row),
                   pl.BlockSpec((e, tm), lambda i: (0, i))],
        out_shape=[jax.ShapeDtypeStruct((r, d), F32), jax.ShapeDtypeStruct((r, d // 2), jnp.uint32),
                   jax.ShapeDtypeStruct((e, r), F32)],
        compiler_params=_params("parallel"),
        name="merge_router",
    )(yf, yb, xs, zs, dskip, w_ssm_norm, w_ssm_out_bf, gc, gates, w_out_bf, h0, w_norm_ffn, w_router_t_bf)


def _select_kernel(aff_ref, idx_ref, wts_ref, rank_ref, *, pad, cap):
    e, tp = aff_ref.shape
    n_hi, n_lo = idx_ref.shape[2], idx_ref.shape[3]
    aff = aff_ref[...]
    tpos = lax.broadcasted_iota(jnp.int32, (e, tp), 1)
    bits = jnp.where(tpos >= pad, pltpu.bitcast(aff, jnp.int32), -1)

    def refine(i, thr):
        cand = thr | lax.shift_left(jnp.int32(1), 30 - i)
        cnt = jnp.sum((bits >= cand).astype(jnp.int32), axis=1, keepdims=True)
        return jnp.where(cnt >= cap, cand, thr)

    thr = lax.fori_loop(0, 31, refine, jnp.zeros((e, 1), jnp.int32))
    gt = bits > thr
    eq = bits == thr
    need = cap - jnp.sum(gt.astype(jnp.int32), axis=1, keepdims=True)

    ri = lax.broadcasted_iota(jnp.int32, (CHUNK, CHUNK), 0)
    ci = lax.broadcasted_iota(jnp.int32, (CHUNK, CHUNK), 1)
    upper = (ri <= ci).astype(BF16)

    def prefix(flags):
        off = jnp.zeros((e, 1), F32)
        for k in range(tp // CHUNK):
            blk = flags[:, k * CHUNK:(k + 1) * CHUNK].astype(BF16)
            inc = jnp.dot(blk, upper, preferred_element_type=F32) + off
            rank_ref[:, k * CHUNK:(k + 1) * CHUNK] = inc
            off = inc[:, CHUNK - 1:CHUNK]

    eqf = jnp.where(eq, 1.0, 0.0)
    prefix(eqf)
    sel = gt | (eq & ((rank_ref[...] - eqf) < need.astype(F32)))
    prefix(jnp.where(sel, 1.0, 0.0))
    rank_ref[...] = jnp.where(sel, rank_ref[...], 0.0) - 1.0

    t1 = lax.broadcasted_iota(jnp.int32, (1, tp), 1)
    t_hi = lax.shift_right_logical(t1, 6).astype(F32)
    t_lo = (t1 & 63).astype(F32)
    hh = lax.broadcasted_iota(jnp.int32, (n_hi, 1), 0).astype(F32)
    ll = lax.broadcasted_iota(jnp.int32, (n_lo, 1), 0).astype(F32)
    jj = lax.broadcasted_iota(jnp.int32, (n_hi, n_lo), 0) * n_lo + lax.broadcasted_iota(jnp.int32, (n_hi, n_lo), 1)

    def emit(ei, carry):
        s0 = rank_ref[pl.ds(ei, 1), :]
        a0 = aff_ref[pl.ds(ei, 1), :]
        hi = jnp.floor(s0 * (1.0 / n_lo))
        lo = s0 - n_lo * hi
        a1 = a0.astype(BF16).astype(F32)
        a2 = (a0 - a1).astype(BF16).astype(F32)
        a3 = (a0 - a1 - a2).astype(BF16).astype(F32)
        in_hi = hi == hh
        parts = [jnp.where(in_hi, v, 0.0) for v in (t_hi, t_lo, a1, a2, a3)]
        parts.append(jnp.zeros((n_hi, tp), F32))
        lhs = jnp.concatenate(parts, axis=0).astype(BF16)
        rhs = jnp.where(lo == ll, 1.0, 0.0).astype(BF16)
        res = lax.dot_general(lhs, rhs, _NT, preferred_element_type=F32)
        tok = (64.0 * res[0:n_hi] + res[n_hi:2 * n_hi]).astype(jnp.int32)
        idx_ref[0, ei] = jnp.where(jj >= cap, jj - cap, tok)
        wts_ref[0, ei] = res[2 * n_hi:3 * n_hi] + res[3 * n_hi:4 * n_hi] + res[4 * n_hi:5 * n_hi]
        return carry

    lax.fori_loop(0, e, emit, 0)


_SEL_HI, _SEL_LO = 8, 128


def _select(aff_t, b, tp, pad, cap):
    e = aff_t.shape[0]
    assert cap <= _SEL_HI * _SEL_LO and tp <= 64 * 256
    return pl.pallas_call(
        functools.partial(_select_kernel, pad=pad, cap=cap),
        grid=(b,),
        in_specs=[pl.BlockSpec((e, tp), lambda i: (0, i))],
        out_specs=[
            pl.BlockSpec((1, e, _SEL_HI, _SEL_LO), lambda i: (i, 0, 0, 0)),
            pl.BlockSpec((1, e, _SEL_HI, _SEL_LO), lambda i: (i, 0, 0, 0)),
        ],
        out_shape=[jax.ShapeDtypeStruct((b, e, _SEL_HI, _SEL_LO), jnp.int32),
                   jax.ShapeDtypeStruct((b, e, _SEL_HI, _SEL_LO), F32)],
        scratch_shapes=[pltpu.VMEM((e, tp), F32)],
        compiler_params=_params("parallel"),
        name="select",
    )(aff_t)


def _gather_kernel(idx_ref, hn_ref, xn_ref, *, n_exp):
    bi, ei = pl.program_id(0), pl.program_id(1)
    capp = xn_ref.shape[2]
    base = (bi * n_exp + ei) * capp

    def copy(j, carry):
        t = idx_ref[base + j]
        xn_ref[0, 0, pl.ds(j, 1), :] = hn_ref[0, pl.ds(t, 1), :]
        return carry

    lax.fori_loop(0, capp, copy, 0, unroll=8)


def _gather(idx_flat, hn_words, n_exp, capp):
    b, tp, dw = hn_words.shape
    return pl.pallas_call(
        functools.partial(_gather_kernel, n_exp=n_exp),
        grid_spec=pltpu.PrefetchScalarGridSpec(
            num_scalar_prefetch=1,
            grid=(b, n_exp),
            in_specs=[pl.BlockSpec((1, tp, dw), lambda i, e, idx: (i, 0, 0))],
            out_specs=pl.BlockSpec((1, 1, capp, dw), lambda i, e, idx: (i, e, 0, 0)),
        ),
        out_shape=jax.ShapeDtypeStruct((b, n_exp, capp, dw), jnp.uint32),
        compiler_params=_params("parallel", "arbitrary"),
        name="moe_gather",
    )(idx_flat, hn_words)


def _pack_bf16_pair(lo, hi):
    lo_bits = pltpu.bitcast(lo.astype(BF16).astype(F32), jnp.uint32)
    hi_bits = pltpu.bitcast(hi.astype(BF16).astype(F32), jnp.uint32)
    return lax.shift_right_logical(lo_bits, jnp.uint32(16)) | hi_bits


def _unpack_bf16_pair(words):
    lo = pltpu.bitcast(lax.shift_left(words, jnp.uint32(16)), F32)
    hi = pltpu.bitcast(words & jnp.uint32(0xFFFF0000), F32)
    return lo.astype(BF16), hi.astype(BF16)


def _ffn_kernel(xw_ref, wts_ref, wg_ref, wu_ref, wd_ref, y_ref, xn_ref):
    f = pl.program_id(2)
    nb, capp, dw = xw_ref.shape[0], xw_ref.shape[2], xw_ref.shape[3]
    d = 2 * dw

    @pl.when(f == 0)
    def _():
        lo, hi = _unpack_bf16_pair(xw_ref[...].reshape(nb * capp, dw))
        xn_ref[:, 0:dw] = lo
        xn_ref[:, dw:d] = hi

    xn = xn_ref[...]
    hg = jnp.dot(xn, wg_ref[0].astype(BF16), preferred_element_type=F32)
    hu = jnp.dot(xn, wu_ref[0].astype(BF16), preferred_element_type=F32)
    act = (_silu(hg) * hu).astype(BF16)
    yp = jnp.dot(act, wd_ref[0].astype(BF16), preferred_element_type=F32).reshape(nb, 1, capp, d)

    @pl.when(f == 0)
    def _():
        y_ref[...] = yp

    @pl.when(f > 0)
    def _():
        y_ref[...] += yp

    @pl.when(f == pl.num_programs(2) - 1)
    def _():
        y_ref[...] = y_ref[...] * wts_ref[...]


def _ffn(xn_words, wts, w_gate, w_up, w_down):
    b, n_exp, capp, dw = xn_words.shape
    d = 2 * dw
    ff = w_gate.shape[2]
    nb = 2 if b % 2 == 0 else 1
    tf = _tile(ff, 512)
    return pl.pallas_call(
        _ffn_kernel,
        grid=(n_exp, b // nb, ff // tf),
        in_specs=[
            pl.BlockSpec((nb, 1, capp, dw), lambda e, m, f: (m, e, 0, 0)),
            pl.BlockSpec((nb, 1, capp, 1), lambda e, m, f: (m, e, 0, 0)),
            pl.BlockSpec((1, d, tf), lambda e, m, f: (e, 0, f)),
            pl.BlockSpec((1, d, tf), lambda e, m, f: (e, 0, f)),
            pl.BlockSpec((1, tf, d), lambda e, m, f: (e, f, 0)),
        ],
        out_specs=pl.BlockSpec((nb, 1, capp, d), lambda e, m, f: (m, e, 0, 0)),
        out_shape=jax.ShapeDtypeStruct((b, n_exp, capp, d), F32),
        scratch_shapes=[pltpu.VMEM((nb * capp, d), BF16)],
        compiler_params=_params("parallel", "parallel", "arbitrary"),
        name="moe_ffn",
    )(xn_words, wts, w_gate, w_up, w_down)


_SCATTER_GROUP = 8


def _scatter_kernel(idx_ref, y_ref, o_ref, *, n_exp):
    bi, ei = pl.program_id(0), pl.program_id(1)
    capp = y_ref.shape[2]
    base = (bi * n_exp + ei) * capp

    @pl.when(ei == 0)
    def _():
        o_ref[...] = jnp.zeros_like(o_ref)

    def add_group(g, carry):
        j0 = pl.multiple_of(g * _SCATTER_GROUP, _SCATTER_GROUP)
        toks = [idx_ref[base + j0 + k] for k in range(_SCATTER_GROUP)]
        rows = [o_ref[0, pl.ds(t, 1), :] for t in toks]
        ytile = y_ref[0, 0, pl.ds(j0, _SCATTER_GROUP), :]
        for k, t in enumerate(toks):
            o_ref[0, pl.ds(t, 1), :] = rows[k] + ytile[k:k + 1, :]
        return carry

    lax.fori_loop(0, capp // _SCATTER_GROUP, add_group, 0)


def _scatter(idx_flat, y, tp):
    b, n_exp, capp, d = y.shape
    return pl.pallas_call(
        functools.partial(_scatter_kernel, n_exp=n_exp),
        grid_spec=pltpu.PrefetchScalarGridSpec(
            num_scalar_prefetch=1,
            grid=(b, n_exp),
            in_specs=[pl.BlockSpec((1, 1, capp, d), lambda i, e, idx: (i, e, 0, 0))],
            out_specs=pl.BlockSpec((1, tp, d), lambda i, e, idx: (i, 0, 0)),
        ),
        out_shape=jax.ShapeDtypeStruct((b, tp, d), F32),
        compiler_params=_params("parallel", "arbitrary"),
        name="moe_scatter",
    )(idx_flat, y)


def _final_kernel(h1_ref, dl_ref, w_ref, o_ref):
    h = h1_ref[...] + dl_ref[...]
    ms = jnp.mean(h * h, axis=-1, keepdims=True)
    o_ref[...] = h * lax.rsqrt(ms + EPS) * w_ref[...]


def _final(h1, delta, w_norm, b):
    r, d = h1.shape
    tp = r // b
    seq = tp - CHUNK
    tm = _tile(seq, 512)
    n_t = seq // tm
    src = pl.BlockSpec((pl.Element(tm), pl.Element(d)),
                       lambda i, c: (pl.multiple_of(i * tp + c * tm + CHUNK, CHUNK), 0))
    return pl.pallas_call(
        _final_kernel,
        grid=(b, n_t),
        in_specs=[src, src, pl.BlockSpec((1, d), lambda i, c: (0, 0))],
        out_specs=pl.BlockSpec((tm, d), lambda i, c: (i * n_t + c, 0)),
        out_shape=jax.ShapeDtypeStruct((b * seq, d), F32),
        compiler_params=_params("parallel", "parallel"),
        name="final_norm",
    )(h1, delta, w_norm).reshape(b, seq, d)


def kernel(x, meta_tokens, w_norm_mix, w_in, w_conf_dw, b_conf_dw, conf_ln_g, conf_ln_b, w_conf_out,
           w_ssm_conv, b_ssm_conv, ssm_dt_bias, ssm_a_log, ssm_d, w_ssm_norm, w_ssm_out, w_out,
           w_norm_ffn, w_router, w_exp_gate, w_exp_up, w_exp_down, w_norm_final):
    b, seq, d = x.shape
    depth = w_in.shape[0]
    assert depth == 1 and seq % CHUNK == 0
    n_meta = meta_tokens.shape[0]
    pad = CHUNK - n_meta
    lt = n_meta + seq
    tp = seq + CHUNK
    r = b * tp
    cw = w_conf_dw.shape[2]
    heads = ssm_d.shape[1]
    inner = heads * SSM_HEAD_DIM
    xw = w_ssm_conv.shape[2]
    n_exp = w_router.shape[2]
    cap = CAPACITY_FACTOR * lt // n_exp
    capp = -(-cap // 8) * 8
    off_z = 2 * cw
    off_xbc = off_z + inner
    off_dt = off_xbc + xw
    off_gate = off_dt + 2 * heads

    w_in0 = w_in[0]
    wa = w_in0[:, 0:cw].astype(BF16)
    wg = w_in0[:, cw:off_z].astype(BF16)
    wz = w_in0[:, off_z:off_xbc].astype(BF16)
    wxbc = w_in0[:, off_xbc:off_dt].astype(BF16)
    wdt_t = w_in0[:, off_dt:off_gate].T.astype(BF16)
    wgate = w_in0[:, off_gate:].astype(BF16)
    dt_bias_col = ssm_dt_bias[0].reshape(2 * heads, 1).astype(F32)
    a_col = (-jnp.exp(ssm_a_log[0].astype(F32))).reshape(2 * heads, 1)
    dskip = jnp.repeat(ssm_d[0].astype(F32), SSM_HEAD_DIM).reshape(1, inner)

    u2, h0 = _prep(x, meta_tokens, w_norm_mix[0])
    u3 = u2.reshape(b, tp, d)
    zs = _mm(u2, wz, "silu", out_dtype=BF16)
    gate_ssm = _mm(u2, wgate[:, d:], "sigmoid", out_dtype=BF16)
    dtt = _dt_proj(u3, wdt_t, dt_bias_col, pad)

    gc = _conf_branch(u3, wa, wg, w_conf_dw[0], b_conf_dw[0], conf_ln_g[0], conf_ln_b[0],
                      w_conf_out[0].astype(BF16), wgate[:, :d])

    xs, xst, bm, cm = _ssm_conv(u3, wxbc, w_ssm_conv[0], b_ssm_conv[0], pad, inner)
    yf = _ssd(xst, dtt, bm, cm, a_col, 0, heads)
    yb = _ssd(xst, dtt, bm, cm, a_col, 1, heads)

    h1, hn_words, aff_t = _merge(yf.reshape(r, inner), yb.reshape(r, inner), xs.reshape(r, inner), zs, dskip,
                           w_ssm_norm[0].reshape(1, inner).astype(F32), w_ssm_out[0].astype(BF16),
                           gc.reshape(r, d), gate_ssm, w_out[0].astype(BF16), h0,
                           w_norm_ffn[0].reshape(1, d).astype(F32), w_router[0].T.astype(BF16))

    idx, wts = _select(aff_t, b, tp, pad, cap)
    idx_flat = idx.reshape(b, n_exp, -1)[:, :, :capp].reshape(b * n_exp * capp)
    wts = wts.reshape(b, n_exp, -1)[:, :, :capp].reshape(b, n_exp, capp, 1)
    xn_words = _gather(idx_flat, hn_words.reshape(b, tp, d // 2), n_exp, capp)
    y = _ffn(xn_words, wts, w_exp_gate[0], w_exp_up[0], w_exp_down[0])
    delta = _scatter(idx_flat, y, tp)
    return _final(h1, delta.reshape(r, d), w_norm_final.reshape(1, d).astype(F32), b)
```
